```python
import jax, jax.numpy as jnp
from jax import lax
import numpy as np

D_MODEL = 1024
BATCH = 2
SEQ = 8192
DEPTH = 4

GRID_W = 64
HEAD_DIM = 64
N_HEADS = 12
N_KV_HEADS = 4
Q_GROUP = N_HEADS // N_KV_HEADS
N_FOURIER_GROUPS = 4
FOURIER_GROUP_W = 64
Q_W = N_HEADS * HEAD_DIM
KV_W = N_KV_HEADS * HEAD_DIM
F_W = N_FOURIER_GROUPS * FOURIER_GROUP_W
N_BRANCHES = 2
GATE_W = N_BRANCHES * D_MODEL
IN_W = Q_W + 2 * KV_W + F_W + GATE_W
D_FF = 4 * D_MODEL
Q_BLOCK = 128
ROPE_THETA = 10000.0
ROPE_PAIRS_PER_AXIS = HEAD_DIM // 4
EPS = 1e-6

kernel_name = "hybrid_gqa_axial_rope_fnet_gated_encoder"


def rmsnorm(x, g):
    xf = x.astype(jnp.float32)
    y = xf * lax.rsqrt(jnp.mean(xf * xf, axis=-1, keepdims=True) + EPS)
    return (y * g.astype(jnp.float32)).astype(x.dtype)


def axial_rope_tables(seq_len):
    n_rows = seq_len // GRID_W
    rows = jnp.repeat(jnp.arange(n_rows, dtype=jnp.float32), GRID_W)
    cols = jnp.tile(jnp.arange(GRID_W, dtype=jnp.float32), n_rows)
    inv_freq = ROPE_THETA ** (-jnp.arange(ROPE_PAIRS_PER_AXIS, dtype=jnp.float32) / ROPE_PAIRS_PER_AXIS)
    ang = jnp.concatenate([rows[:, None] * inv_freq[None, :],
                           cols[:, None] * inv_freq[None, :]], axis=-1)
    return jnp.cos(ang), jnp.sin(ang)


def apply_rope(x, cos, sin):
    xp = x.reshape(*x.shape[:-1], HEAD_DIM // 2, 2)
    x0, x1 = xp[..., 0], xp[..., 1]
    c = cos[None, :, None, :].astype(x.dtype)
    s = sin[None, :, None, :].astype(x.dtype)
    return jnp.stack([x0 * c - x1 * s, x0 * s + x1 * c], axis=-1).reshape(x.shape)


def blocked_gqa(q, k, v):
    B, S, _, _ = q.shape
    n_blk = S // Q_BLOCK
    scale = HEAD_DIM ** -0.5
    qg = q.reshape(B, n_blk, Q_BLOCK, N_KV_HEADS, Q_GROUP, HEAD_DIM)
    qg = jnp.transpose(qg, (1, 0, 3, 4, 2, 5))
    kt = jnp.transpose(k, (0, 2, 1, 3))
    vt = jnp.transpose(v, (0, 2, 1, 3))

    def one_block(qb):
        s = jnp.einsum('bkgqd,bksd->bkgqs', qb, kt, preferred_element_type=jnp.float32) * scale
        p = jax.nn.softmax(s, axis=-1)
        return jnp.einsum('bkgqs,bksd->bkgqd', p.astype(vt.dtype), vt)

    o = lax.map(one_block, qg)
    o = jnp.transpose(o, (1, 0, 4, 2, 3, 5))
    return o.reshape(B, S, Q_W)


def fourier_mix(f):
    B, S, _ = f.shape
    fg = f.reshape(B, S, N_FOURIER_GROUPS, FOURIER_GROUP_W).astype(jnp.float32)
    out = jnp.fft.fft2(fg, axes=(1, 3), norm="ortho").real
    return out.reshape(B, S, F_W).astype(f.dtype)


def mixer_block(h, w_in, b_gate, q_gain, k_gain, w_attn_branch, w_fourier_branch, w_out, cos, sin):
    B, S, _ = h.shape
    proj = h @ w_in
    q, k, v, f, gate_logits = jnp.split(
        proj, [Q_W, Q_W + KV_W, Q_W + 2 * KV_W, Q_W + 2 * KV_W + F_W], axis=-1)
    q = apply_rope(rmsnorm(q.reshape(B, S, N_HEADS, HEAD_DIM), q_gain), cos, sin)
    k = apply_rope(rmsnorm(k.reshape(B, S, N_KV_HEADS, HEAD_DIM), k_gain), cos, sin)
    v = v.reshape(B, S, N_KV_HEADS, HEAD_DIM)
    attn = blocked_gqa(q, k, v)
    four = fourier_mix(f)
    gates = jax.nn.sigmoid((gate_logits + b_gate).astype(jnp.float32)).astype(h.dtype)
    g_attn, g_four = jnp.split(gates, N_BRANCHES, axis=-1)
    merged = g_attn * (attn @ w_attn_branch) + g_four * (four @ w_fourier_branch)
    return merged @ w_out


def squared_relu_mlp(h, w_up, w_down):
    u = jax.nn.relu(h @ w_up)
    return (u * u) @ w_down


def setup_inputs(seed: int = 0) -> dict:
    key = jax.random.key(seed)
    ks = jax.random.split(key, 13)
    f32 = jnp.float32

    def nrm(k, shape, fan_in):
        return jax.random.normal(k, shape, f32) * (fan_in ** -0.5)

    def gain(k, shape):
        return 1.0 + 0.02 * jax.random.normal(k, shape, f32)

    return {
        "x": jax.random.normal(ks[0], (BATCH, SEQ, D_MODEL), f32),
        "norm_mix": gain(ks[1], (DEPTH, D_MODEL)),
        "w_in": nrm(ks[2], (DEPTH, D_MODEL, IN_W), D_MODEL),
        "b_gate": 0.01 * jax.random.normal(ks[3], (DEPTH, GATE_W), f32),
        "q_gain": gain(ks[4], (DEPTH, HEAD_DIM)),
        "k_gain": gain(ks[5], (DEPTH, HEAD_DIM)),
        "w_attn_branch": nrm(ks[6], (DEPTH, Q_W, D_MODEL), Q_W),
        "w_fourier_branch": nrm(ks[7], (DEPTH, F_W, D_MODEL), F_W),
        "w_out": nrm(ks[8], (DEPTH, D_MODEL, D_MODEL), D_MODEL),
        "norm_mlp": gain(ks[9], (DEPTH, D_MODEL)),
        "w_up": nrm(ks[10], (DEPTH, D_MODEL, D_FF), D_MODEL),
        "w_down": nrm(ks[11], (DEPTH, D_FF, D_MODEL), D_FF),
        "norm_final": gain(ks[12], (D_MODEL,)),
    }


def reference(x, norm_mix, w_in, b_gate, q_gain, k_gain, w_attn_branch, w_fourier_branch,
              w_out, norm_mlp, w_up, w_down, norm_final):
    cos, sin = axial_rope_tables(x.shape[1])
    for l in range(DEPTH):
        h = rmsnorm(x, norm_mix[l])
        x = x + mixer_block(h, w_in[l], b_gate[l], q_gain[l], k_gain[l], w_attn_branch[l],
                            w_fourier_branch[l], w_out[l], cos, sin)
        h = rmsnorm(x, norm_mlp[l])
        x = x + squared_relu_mlp(h, w_up[l], w_down[l])
    return rmsnorm(x, norm_final)
```

```python
import functools
import math

import numpy as np
import jax
import jax.numpy as jnp
from jax import lax
from jax.experimental import pallas as pl
from jax.experimental.pallas import tpu as pltpu

HEAD_DIM = 64
N_HEADS = 12
N_KV_HEADS = 4
Q_GROUP = N_HEADS // N_KV_HEADS
FOURIER_GROUP_W = 64
N_FOURIER_GROUPS = 4
Q_W = N_HEADS * HEAD_DIM
KV_W = N_KV_HEADS * HEAD_DIM
F_W = N_FOURIER_GROUPS * FOURIER_GROUP_W
GRID_W = 64
ROPE_THETA = 10000.0
ROPE_PAIRS_PER_AXIS = HEAD_DIM // 4
EPS = 1e-6
HALF = HEAD_DIM // 2

V7X_VMEM_LIMIT_BYTES = 56 * 1024 * 1024
ONES_ROWS = 16
Q_SCALE = HEAD_DIM ** -0.5 * math.log2(math.e)

BF16 = jnp.bfloat16
F32 = jnp.float32


def _tiles(seq):
    tm = min(512, seq)
    tk = min(256, seq)
    tq = min(512, seq)
    n1 = 64
    n2 = seq // n1
    assert seq % tm == 0 and tm % tk == 0 and seq % tq == 0 and n1 * n2 == seq
    assert (seq // tk) % 2 == 0
    return tm, tk, tq, n1, n2


def _params(*sem):
    return pltpu.CompilerParams(dimension_semantics=sem, vmem_limit_bytes=V7X_VMEM_LIMIT_BYTES)


def _proj_kernel(x_ref, g_ref, wqkv_ref, wfg_ref, bg_ref, qg_ref, kg_ref, cos_ref, sin_ref, cs_ref,
                 q_out, k_out, v_out, zr_out, zi_out, gate_out, *, tk):
    x = x_ref[0]
    tm = x.shape[0]
    ms = jnp.mean(x * x, axis=-1, keepdims=True)
    h = (x * lax.rsqrt(ms + EPS) * g_ref[...]).astype(BF16)

    qkv_t = lax.dot_general(wqkv_ref[...], h, (((1,), (1,)), ((), ())),
                            preferred_element_type=F32)
    cos = cos_ref[...][None]
    sin = sin_ref[...][None]

    def norm_rope(t, gain, n_heads, scale):
        t = t.reshape(n_heads, HEAD_DIM, tm)
        msq = jnp.mean(t * t, axis=1, keepdims=True)
        y = t * lax.rsqrt(msq + EPS) * gain[None]
        a, b = y[:, :HALF], y[:, HALF:]
        out = jnp.concatenate([a * cos - b * sin, a * sin + b * cos], axis=1)
        return out * scale if scale != 1.0 else out

    q = norm_rope(qkv_t[:Q_W], qg_ref[...], N_HEADS, Q_SCALE)
    q_out[0] = q.reshape(N_KV_HEADS, Q_GROUP, HEAD_DIM, tm).astype(BF16)

    k = norm_rope(qkv_t[Q_W:Q_W + KV_W], kg_ref[...], N_KV_HEADS, 1.0)
    k_rows = k.reshape(KV_W, tm).T.astype(BF16)
    for hk in range(N_KV_HEADS):
        k_out[0, hk] = k_rows[:, hk * HEAD_DIM:(hk + 1) * HEAD_DIM]

    v_t = qkv_t[Q_W + KV_W:].astype(BF16).reshape(N_KV_HEADS, HEAD_DIM, tm)
    for c in range(tm // tk):
        v_out[0, :, c] = v_t[:, :, c * tk:(c + 1) * tk]

    fg = jnp.dot(h, wfg_ref[...], preferred_element_type=F32)
    z = jnp.dot(fg[:, :F_W].astype(BF16), cs_ref[...], preferred_element_type=F32)
    zr_out[0] = z[:, :F_W]
    zi_out[0] = z[:, F_W:]
    gate_out[0] = jax.nn.sigmoid(fg[:, F_W:] + bg_ref[...]).astype(BF16)


def _proj_call(x, g, wqkv_t, wfg, bg, qg, kg, cos_t, sin_t, cs, *, tm, tk):
    B, S, D = x.shape
    n_rows = wqkv_t.shape[0]
    gate_w = wfg.shape[1] - F_W
    full = lambda shape: pl.BlockSpec(shape, lambda b, i: (0,) * len(shape))
    return pl.pallas_call(
        functools.partial(_proj_kernel, tk=tk),
        grid=(B, S // tm),
        in_specs=[
            pl.BlockSpec((1, tm, D), lambda b, i: (b, i, 0)),
            full((1, D)),
            full((n_rows, D)),
            full((D, F_W + gate_w)),
            full((1, gate_w)),
            full((HEAD_DIM, 1)),
            full((HEAD_DIM, 1)),
            pl.BlockSpec((HALF, tm), lambda b, i: (0, i)),
            pl.BlockSpec((HALF, tm), lambda b, i: (0, i)),
            full((F_W, 2 * F_W)),
        ],
        out_specs=[
            pl.BlockSpec((1, N_KV_HEADS, Q_GROUP, HEAD_DIM, tm), lambda b, i: (b, 0, 0, 0, i)),
            pl.BlockSpec((1, N_KV_HEADS, tm, HEAD_DIM), lambda b, i: (b, 0, i, 0)),
            pl.BlockSpec((1, N_KV_HEADS, tm // tk, HEAD_DIM, tk), lambda b, i: (b, 0, i, 0, 0)),
            pl.BlockSpec((1, tm, F_W), lambda b, i: (b, i, 0)),
            pl.BlockSpec((1, tm, F_W), lambda b, i: (b, i, 0)),
            pl.BlockSpec((1, tm, gate_w), lambda b, i: (b, i, 0)),
        ],
        out_shape=[
            jax.ShapeDtypeStruct((B, N_KV_HEADS, Q_GROUP, HEAD_DIM, S), BF16),
            jax.ShapeDtypeStruct((B, N_KV_HEADS, S, HEAD_DIM), BF16),
            jax.ShapeDtypeStruct((B, N_KV_HEADS, S // tk, HEAD_DIM, tk), BF16),
            jax.ShapeDtypeStruct((B, S, F_W), F32),
            jax.ShapeDtypeStruct((B, S, F_W), F32),
            jax.ShapeDtypeStruct((B, S, gate_w), BF16),
        ],
        compiler_params=_params("parallel", "parallel"),
        name="proj",
    )(x, g, wqkv_t, wfg, bg, qg, kg, cos_t, sin_t, cs)


def _attn_kernel(q_ref, k_ref, v_ref, o_ref, s_buf, cmax_ref, m_ref, acc_ref, *, tk, n_chunks):
    tq = q_ref.shape[-1]
    q_cat = jnp.concatenate([q_ref[0, 0, g] for g in range(Q_GROUP)], axis=1)
    ones = jnp.ones((ONES_ROWS, tk), BF16)

    def scores(j, slot):
        start = pl.multiple_of(j * tk, tk)
        kc = k_ref[0, 0, pl.ds(start, tk), :]
        s_t = jnp.dot(kc, q_cat, preferred_element_type=F32)
        s_buf[slot] = s_t
        cmax_ref[slot] = jnp.max(s_t, axis=0, keepdims=True)

    def accumulate(j, slot):
        m_old = m_ref[...]
        m_new = jnp.maximum(m_old, cmax_ref[slot])
        alpha = jnp.exp2(m_old - m_new)
        p_t = jnp.exp2(s_buf[slot] - m_new).astype(BF16)
        v_ext = jnp.concatenate([v_ref[0, 0, j], ones], axis=0)
        acc_ref[...] = acc_ref[...] * alpha + jnp.dot(v_ext, p_t, preferred_element_type=F32)
        m_ref[...] = m_new

    m_ref[...] = jnp.full(m_ref.shape, -jnp.inf, F32)
    acc_ref[...] = jnp.zeros(acc_ref.shape, F32)
    scores(0, 0)

    def two_chunks(jj, carry):
        j0 = 2 * jj
        scores(j0 + 1, 1)
        accumulate(j0, 0)
        scores(jnp.minimum(j0 + 2, n_chunks - 1), 0)
        accumulate(j0 + 1, 1)
        return carry

    lax.fori_loop(0, n_chunks // 2, two_chunks, 0)
    acc = acc_ref[...]
    out = acc[:HEAD_DIM] / acc[HEAD_DIM:HEAD_DIM + 1]
    for g in range(Q_GROUP):
        o_ref[0, 0, g] = out[:, g * tq:(g + 1) * tq].astype(o_ref.dtype)


def _attn_call(q_t, k, v_t, *, tq):
    B, _, _, _, S = q_t.shape
    n_chunks, tk = v_t.shape[2], v_t.shape[4]
    return pl.pallas_call(
        functools.partial(_attn_kernel, tk=tk, n_chunks=n_chunks),
        grid=(B, N_KV_HEADS, S // tq),
        in_specs=[
            pl.BlockSpec((1, 1, Q_GROUP, HEAD_DIM, tq), lambda b, h, i: (b, h, 0, 0, i)),
            pl.BlockSpec((1, 1, S, HEAD_DIM), lambda b, h, i: (b, h, 0, 0)),
            pl.BlockSpec((1, 1, n_chunks, HEAD_DIM, tk), lambda b, h, i: (b, h, 0, 0, 0)),
        ],
        out_specs=pl.BlockSpec((1, 1, Q_GROUP, HEAD_DIM, tq), lambda b, h, i: (b, h, 0, 0, i)),
        out_shape=jax.ShapeDtypeStruct((B, N_KV_HEADS, Q_GROUP, HEAD_DIM, S), BF16),
        scratch_shapes=[
            pltpu.VMEM((2, tk, Q_GROUP * tq), F32),
            pltpu.VMEM((2, 1, Q_GROUP * tq), F32),
            pltpu.VMEM((1, Q_GROUP * tq), F32),
            pltpu.VMEM((HEAD_DIM + ONES_ROWS, Q_GROUP * tq), F32),
        ],
        compiler_params=_params("parallel", "parallel", "parallel"),
        name="attn",
    )(q_t, k, v_t)


def _fft1_kernel(zr_ref, zi_ref, c1_ref, s1_ref, tc_ref, ts_ref, yr_out, yi_out):
    zr = zr_ref[0].astype(BF16)
    zi = zi_ref[0].astype(BF16)
    c1, s1 = c1_ref[...], s1_ref[...]
    dot = functools.partial(jnp.dot, preferred_element_type=F32)
    yr = dot(c1, zr) + dot(s1, zi)
    yi = dot(c1, zi) - dot(s1, zr)
    tc, ts = tc_ref[...], ts_ref[...]
    yr_out[0] = yr * tc + yi * ts
    yi_out[0] = yi * tc - yr * ts


def _fft1_call(zr, zi, c1, s1, tc, ts, *, lanes):
    B, n1, L = zr.shape
    blk = pl.BlockSpec((1, n1, lanes), lambda b, i: (b, 0, i))
    tw = pl.BlockSpec((n1, lanes), lambda b, i: (0, i))
    full = pl.BlockSpec((n1, n1), lambda b, i: (0, 0))
    return pl.pallas_call(
        _fft1_kernel,
        grid=(B, L // lanes),
        in_specs=[blk, blk, full, full, tw, tw],
        out_specs=[blk, blk],
        out_shape=[jax.ShapeDtypeStruct((B, n1, L), F32)] * 2,
        compiler_params=_params("parallel", "parallel"),
        name="fft_stage1",
    )(zr, zi, c1, s1, tc, ts)


def _fft2_kernel(yr_ref, yi_ref, c2_ref, s2_ref, o_ref, *, group):
    c2, s2 = c2_ref[...], s2_ref[...]
    width = yr_ref.shape[-1]
    for g in range(group):
        yr = yr_ref[0, g].astype(BF16)
        yi = yi_ref[0, g].astype(BF16)
        re = (jnp.dot(c2, yr, preferred_element_type=F32)
              + jnp.dot(s2, yi, preferred_element_type=F32))
        o_ref[0, :, g * width:(g + 1) * width] = re.astype(o_ref.dtype)


def _fft2_call(yr, yi, c2, s2, *, group):
    B, n1, n2, C = yr.shape
    blk = pl.BlockSpec((1, group, n2, C), lambda b, i: (b, i, 0, 0))
    full = pl.BlockSpec((n2, n2), lambda b, i: (0, 0))
    return pl.pallas_call(
        functools.partial(_fft2_kernel, group=group),
        grid=(B, n1 // group),
        in_specs=[blk, blk, full, full],
        out_specs=pl.BlockSpec((1, n2, group * C), lambda b, i: (b, 0, i)),
        out_shape=jax.ShapeDtypeStruct((B, n2, n1 * C), BF16),
        compiler_params=_params("parallel", "parallel"),
        name="fft_stage2",
    )(yr, yi, c2, s2)


def _merge_kernel(x_ref, at_ref, four_ref, gate_ref, wab_ref, wfb_ref, wout_ref, o_ref):
    D = x_ref.shape[-1]
    a = lax.dot_general(at_ref[0], wab_ref[...], (((0,), (0,)), ((), ())),
                        preferred_element_type=F32)
    f = jnp.dot(four_ref[0], wfb_ref[...], preferred_element_type=F32)
    gates = gate_ref[0].astype(F32)
    merged = (gates[:, :D] * a + gates[:, D:] * f).astype(BF16)
    o_ref[0] = x_ref[0] + jnp.dot(merged, wout_ref[...], preferred_element_type=F32)


def _merge_call(x, attn_t, four, gates, wab, wfb, wout, *, tm):
    B, S, D = x.shape
    full = lambda shape: pl.BlockSpec(shape, lambda b, i: (0,) * len(shape))
    row = lambda w: pl.BlockSpec((1, tm, w), lambda b, i: (b, i, 0))
    return pl.pallas_call(
        _merge_kernel,
        grid=(B, S // tm),
        in_specs=[
            row(D),
            pl.BlockSpec((1, Q_W, tm), lambda b, i: (b, 0, i)),
            row(F_W),
            row(2 * D),
            full((Q_W, D)), full((F_W, D)), full((D, D)),
        ],
        out_specs=row(D),
        out_shape=jax.ShapeDtypeStruct((B, S, D), F32),
        compiler_params=_params("parallel", "parallel"),
        name="merge",
    )(x, attn_t, four, gates, wab, wfb, wout)


def _mlp_kernel(x_ref, g_ref, wup_ref, wdown_ref, gf_ref, o_ref, *, ff_chunk, final_norm):
    x = x_ref[0]
    ms = jnp.mean(x * x, axis=-1, keepdims=True)
    h = (x * lax.rsqrt(ms + EPS) * g_ref[...]).astype(BF16)
    d_ff = wup_ref.shape[1]
    y = x
    for c in range(d_ff // ff_chunk):
        u = jnp.maximum(jnp.dot(h, wup_ref[:, c * ff_chunk:(c + 1) * ff_chunk],
                                preferred_element_type=F32), 0.0)
        y = y + jnp.dot((u * u).astype(BF16), wdown_ref[c * ff_chunk:(c + 1) * ff_chunk, :],
                        preferred_element_type=F32)
    if final_norm:
        ms = jnp.mean(y * y, axis=-1, keepdims=True)
        y = y * lax.rsqrt(ms + EPS) * gf_ref[...]
    o_ref[0] = y


def _mlp_call(x, g, wup, wdown, gf, *, tm, final_norm):
    B, S, D = x.shape
    d_ff = wup.shape[1]
    full = lambda shape: pl.BlockSpec(shape, lambda b, i: (0,) * len(shape))
    row = pl.BlockSpec((1, tm, D), lambda b, i: (b, i, 0))
    return pl.pallas_call(
        functools.partial(_mlp_kernel, ff_chunk=min(1024, d_ff), final_norm=final_norm),
        grid=(B, S // tm),
        in_specs=[row, full((1, D)), full((D, d_ff)), full((d_ff, D)), full((1, D))],
        out_specs=row,
        out_shape=jax.ShapeDtypeStruct((B, S, D), F32),
        compiler_params=_params("parallel", "parallel"),
        name="mlp",
    )(x, g, wup, wdown, gf)


def _rope_tables_t(seq):
    n_rows = seq // GRID_W
    rows = jnp.repeat(jnp.arange(n_rows, dtype=F32), GRID_W)
    cols = jnp.tile(jnp.arange(GRID_W, dtype=F32), n_rows)
    inv_freq = ROPE_THETA ** (-jnp.arange(ROPE_PAIRS_PER_AXIS, dtype=F32) / ROPE_PAIRS_PER_AXIS)
    ang = jnp.concatenate([rows[:, None] * inv_freq[None, :], cols[:, None] * inv_freq[None, :]], axis=-1)
    return jnp.cos(ang).T, jnp.sin(ang).T


def _dft_cos_sin(n, scale):
    idx = np.arange(n)
    ang = 2.0 * np.pi * ((idx[:, None] * idx[None, :]) % n) / n
    return np.cos(ang) * scale, np.sin(ang) * scale


def _fourier_constants(seq, n1, n2):
    gc, gs = _dft_cos_sin(FOURIER_GROUP_W, FOURIER_GROUP_W ** -0.5)
    eye = np.eye(N_FOURIER_GROUPS)
    cs = np.concatenate([np.kron(eye, gc), -np.kron(eye, gs)], axis=1)
    c1, s1 = _dft_cos_sin(n1, n1 ** -0.5)
    c2, s2 = _dft_cos_sin(n2, n2 ** -0.5)
    k1 = np.arange(n1)[:, None]
    s2_idx = np.arange(n2)[None, :]
    ang = 2.0 * np.pi * ((k1 * s2_idx) % seq) / seq
    tc = np.repeat(np.cos(ang), F_W, axis=1)
    ts = np.repeat(np.sin(ang), F_W, axis=1)
    f32 = lambda a: jnp.asarray(a, dtype=F32)
    bf = lambda a: f32(a).astype(BF16)
    return bf(cs), bf(c1), bf(s1), bf(c2), bf(s2), f32(tc), f32(ts)


def _pair_split_perm():
    return np.concatenate([np.arange(0, HEAD_DIM, 2), np.arange(1, HEAD_DIM, 2)])


def kernel(x, norm_mix, w_in, b_gate, q_gain, k_gain, w_attn_branch, w_fourier_branch, w_out,
           norm_mlp, w_up, w_down, norm_final):
    B, S, D = x.shape
    depth = w_in.shape[0]
    tm, tk, tq, n1, n2 = _tiles(S)

    perm = _pair_split_perm()
    q_cols = (np.arange(N_HEADS)[:, None] * HEAD_DIM + perm[None, :]).reshape(-1)
    k_cols = Q_W + (np.arange(N_KV_HEADS)[:, None] * HEAD_DIM + perm[None, :]).reshape(-1)
    v_cols = Q_W + KV_W + np.arange(KV_W)
    qkv_cols = np.concatenate([q_cols, k_cols, v_cols])
    fg_start = Q_W + 2 * KV_W

    cos_t, sin_t = _rope_tables_t(S)
    cs, c1, s1, c2, s2, tc, ts = _fourier_constants(S, n1, n2)

    for l in range(depth):
        wqkv_t = w_in[l][:, qkv_cols].T.astype(BF16)
        wfg = w_in[l][:, fg_start:].astype(BF16)
        qg = q_gain[l][perm][:, None]
        kg = k_gain[l][perm][:, None]
        q_t, k, v_t, zr, zi, gates = _proj_call(
            x, norm_mix[l][None], wqkv_t, wfg, b_gate[l][None], qg, kg, cos_t, sin_t, cs, tm=tm, tk=tk)

        attn_t = _attn_call(q_t, k, v_t, tq=tq)
        attn_t = attn_t.reshape(B, Q_W, S)

        yr, yi = _fft1_call(zr.reshape(B, n1, n2 * F_W), zi.reshape(B, n1, n2 * F_W), c1, s1, tc, ts,
                            lanes=min(4096, n2 * F_W))
        four = _fft2_call(yr.reshape(B, n1, n2, F_W), yi.reshape(B, n1, n2, F_W), c2, s2,
                          group=min(8, n1))
        four = four.reshape(B, S, F_W)

        x = _merge_call(x, attn_t, four, gates, w_attn_branch[l].astype(BF16),
                        w_fourier_branch[l].astype(BF16), w_out[l].astype(BF16), tm=tm)
        x = _mlp_call(x, norm_mlp[l][None], w_up[l].astype(BF16), w_down[l].astype(BF16),
                      norm_final[None], tm=tm, final_norm=(l == depth - 1))
    return x
```

```python
import functools
import math

import numpy as np
import jax
import jax.numpy as jnp
from jax import lax
from jax.experimental import pallas as pl
from jax.experimental.pallas import tpu as pltpu

HEAD_DIM = 64
N_HEADS = 12
N_KV_HEADS = 4
Q_GROUP = N_HEADS // N_KV_HEADS
FOURIER_GROUP_W = 64
N_FOURIER_GROUPS = 4
Q_W = N_HEADS * HEAD_DIM
KV_W = N_KV_HEADS * HEAD_DIM
F_W = N_FOURIER_GROUPS * FOURIER_GROUP_W
GRID_W = 64
ROPE_THETA = 10000.0
ROPE_PAIRS_PER_AXIS = HEAD_DIM // 4
EPS = 1e-6
HALF = HEAD_DIM // 2

V7X_VMEM_LIMIT_BYTES = 56 * 1024 * 1024
ONES_ROWS = 16
Q_SCALE = HEAD_DIM ** -0.5 * math.log2(math.e)
SAFE_SCORE_BOUND = 48.0
SAFE_VALUE_BOUND = 2.0 ** 40
BOUNDED_CHUNKS_PER_TRIP = 8
FFT1_SUBLANE_BLOCK = 8

BF16 = jnp.bfloat16
F32 = jnp.float32


def _tiles(seq):
    tm = min(512, seq)
    tk = min(256, seq)
    tq = min(512, seq)
    n1 = 64
    n2 = seq // n1
    assert seq % tm == 0 and tm % tk == 0 and seq % tq == 0 and n1 * n2 == seq
    assert (seq // tk) % 2 == 0
    return tm, tk, tq, n1, n2


def _params(*sem):
    return pltpu.CompilerParams(dimension_semantics=sem, vmem_limit_bytes=V7X_VMEM_LIMIT_BYTES)


def _proj_kernel(x_ref, g_ref, wqkv_ref, wfg_ref, bg_ref, qg_ref, kg_ref, cos_ref, sin_ref, cs_ref,
                 q_out, k_out, kn_out, v_out, zr_out, zi_out, gate_out, *, tk):
    x = x_ref[0]
    tm = x.shape[0]
    ms = jnp.mean(x * x, axis=-1, keepdims=True)
    h = (x * lax.rsqrt(ms + EPS) * g_ref[...]).astype(BF16)

    qkv_t = lax.dot_general(wqkv_ref[...], h, (((1,), (1,)), ((), ())),
                            preferred_element_type=F32)
    cos = cos_ref[...][None]
    sin = sin_ref[...][None]

    def norm_rope(t, gain, n_heads, scale):
        t = t.reshape(n_heads, HEAD_DIM, tm)
        msq = jnp.mean(t * t, axis=1, keepdims=True)
        y = t * lax.rsqrt(msq + EPS) * gain[None]
        a, b = y[:, :HALF], y[:, HALF:]
        out = jnp.concatenate([a * cos - b * sin, a * sin + b * cos], axis=1)
        return out * scale if scale != 1.0 else out

    q = norm_rope(qkv_t[:Q_W], qg_ref[...], N_HEADS, Q_SCALE)
    q_out[0] = q.reshape(N_KV_HEADS, Q_GROUP, HEAD_DIM, tm).astype(BF16)

    k = norm_rope(qkv_t[Q_W:Q_W + KV_W], kg_ref[...], N_KV_HEADS, 1.0)
    k_used = k.astype(BF16).astype(F32)
    kn_out[0, :, 0:1] = jnp.sum(k_used * k_used, axis=1, keepdims=True)
    k_rows = k.reshape(KV_W, tm).T.astype(BF16)
    for hk in range(N_KV_HEADS):
        k_out[0, hk] = k_rows[:, hk * HEAD_DIM:(hk + 1) * HEAD_DIM]

    v_t = qkv_t[Q_W + KV_W:].astype(BF16).reshape(N_KV_HEADS, HEAD_DIM, tm)
    kn_out[0, :, 1:2] = jnp.max(jnp.abs(v_t.astype(F32)), axis=1, keepdims=True)
    for c in range(tm // tk):
        v_out[0, :, c] = v_t[:, :, c * tk:(c + 1) * tk]

    fg = jnp.dot(h, wfg_ref[...], preferred_element_type=F32)
    z = jnp.dot(fg[:, :F_W].astype(BF16), cs_ref[...], preferred_element_type=F32)
    zr_out[0] = z[:, :F_W]
    zi_out[0] = z[:, F_W:]
    gate_out[0] = jax.nn.sigmoid(fg[:, F_W:] + bg_ref[...]).astype(BF16)


def _proj_call(x, g, wqkv_t, wfg, bg, qg, kg, cos_t, sin_t, cs, *, tm, tk):
    B, S, D = x.shape
    n_rows = wqkv_t.shape[0]
    gate_w = wfg.shape[1] - F_W
    full = lambda shape: pl.BlockSpec(shape, lambda b, i: (0,) * len(shape))
    return pl.pallas_call(
        functools.partial(_proj_kernel, tk=tk),
        grid=(B, S // tm),
        in_specs=[
            pl.BlockSpec((1, tm, D), lambda b, i: (b, i, 0)),
            full((1, D)),
            full((n_rows, D)),
            full((D, F_W + gate_w)),
            full((1, gate_w)),
            full((HEAD_DIM, 1)),
            full((HEAD_DIM, 1)),
            pl.BlockSpec((HALF, tm), lambda b, i: (0, i)),
            pl.BlockSpec((HALF, tm), lambda b, i: (0, i)),
            full((F_W, 2 * F_W)),
        ],
        out_specs=[
            pl.BlockSpec((1, N_KV_HEADS, Q_GROUP, HEAD_DIM, tm), lambda b, i: (b, 0, 0, 0, i)),
            pl.BlockSpec((1, N_KV_HEADS, tm, HEAD_DIM), lambda b, i: (b, 0, i, 0)),
            pl.BlockSpec((1, N_KV_HEADS, 2, tm), lambda b, i: (b, 0, 0, i)),
            pl.BlockSpec((1, N_KV_HEADS, tm // tk, HEAD_DIM, tk), lambda b, i: (b, 0, i, 0, 0)),
            pl.BlockSpec((1, tm, F_W), lambda b, i: (b, i, 0)),
            pl.BlockSpec((1, tm, F_W), lambda b, i: (b, i, 0)),
            pl.BlockSpec((1, tm, gate_w), lambda b, i: (b, i, 0)),
        ],
        out_shape=[
            jax.ShapeDtypeStruct((B, N_KV_HEADS, Q_GROUP, HEAD_DIM, S), BF16),
            jax.ShapeDtypeStruct((B, N_KV_HEADS, S, HEAD_DIM), BF16),
            jax.ShapeDtypeStruct((B, N_KV_HEADS, 2, S), F32),
            jax.ShapeDtypeStruct((B, N_KV_HEADS, S // tk, HEAD_DIM, tk), BF16),
            jax.ShapeDtypeStruct((B, S, F_W), F32),
            jax.ShapeDtypeStruct((B, S, F_W), F32),
            jax.ShapeDtypeStruct((B, S, gate_w), BF16),
        ],
        compiler_params=_params("parallel", "parallel"),
        name="proj",
    )(x, g, wqkv_t, wfg, bg, qg, kg, cos_t, sin_t, cs)


def _attn_kernel(q_ref, k_ref, v_ref, kn_ref, o_ref, s_buf, cmax_ref, m_ref, acc_ref, *, tk, n_chunks):
    tq = q_ref.shape[-1]
    q_cat = jnp.concatenate([q_ref[0, 0, g] for g in range(Q_GROUP)], axis=1)
    ones = jnp.ones((ONES_ROWS, tk), BF16)

    def scores(j, slot):
        start = pl.multiple_of(j * tk, tk)
        kc = k_ref[0, 0, pl.ds(start, tk), :]
        s_t = jnp.dot(kc, q_cat, preferred_element_type=F32)
        s_buf[slot] = s_t
        cmax_ref[slot] = jnp.max(s_t, axis=0, keepdims=True)

    def accumulate(j, slot):
        m_old = m_ref[...]
        m_new = jnp.maximum(m_old, cmax_ref[slot])
        alpha = jnp.exp2(m_old - m_new)
        p_t = jnp.exp2(s_buf[slot] - m_new).astype(BF16)
        v_ext = jnp.concatenate([v_ref[0, 0, j], ones], axis=0)
        acc_ref[...] = acc_ref[...] * alpha + jnp.dot(v_ext, p_t, preferred_element_type=F32)
        m_ref[...] = m_new

    def general_path():
        m_ref[...] = jnp.full(m_ref.shape, -jnp.inf, F32)
        scores(0, 0)

        def two_chunks(jj, carry):
            j0 = 2 * jj
            scores(j0 + 1, 1)
            accumulate(j0, 0)
            scores(jnp.minimum(j0 + 2, n_chunks - 1), 0)
            accumulate(j0 + 1, 1)
            return carry

        lax.fori_loop(0, n_chunks // 2, two_chunks, 0)

    def bounded_path():
        per_trip = math.gcd(BOUNDED_CHUNKS_PER_TRIP, n_chunks)

        def trip(jj, carry):
            p_parts, v_parts = [], []
            for c in range(per_trip):
                j = jj * per_trip + c
                start = pl.multiple_of(j * tk, tk)
                kc = k_ref[0, 0, pl.ds(start, tk), :]
                p_parts.append(jnp.exp2(jnp.dot(kc, q_cat, preferred_element_type=F32)).astype(BF16))
                v_parts.append(jnp.concatenate([v_ref[0, 0, j], ones], axis=0))
            p_t = jnp.concatenate(p_parts, axis=0)
            v_ext = jnp.concatenate(v_parts, axis=1)
            acc_ref[...] += jnp.dot(v_ext, p_t, preferred_element_type=F32)
            return carry

        lax.fori_loop(0, n_chunks // per_trip, trip, 0)

    acc_ref[...] = jnp.zeros(acc_ref.shape, F32)
    q32 = q_cat.astype(F32)
    q_norm2 = jnp.max(jnp.sum(q32 * q32, axis=0, keepdims=True))
    k_norm2 = jnp.max(kn_ref[0, 0, 0:1])
    v_max = jnp.max(kn_ref[0, 0, 1:2])
    bounded = jnp.logical_and(q_norm2 * k_norm2 <= SAFE_SCORE_BOUND ** 2, v_max <= SAFE_VALUE_BOUND)
    lax.cond(bounded, bounded_path, general_path)
    acc = acc_ref[...]
    out = acc[:HEAD_DIM] / acc[HEAD_DIM:HEAD_DIM + 1]
    for g in range(Q_GROUP):
        o_ref[0, 0, g] = out[:, g * tq:(g + 1) * tq].astype(o_ref.dtype)


def _attn_call(q_t, k, v_t, k_norm2, *, tq):
    B, _, _, _, S = q_t.shape
    n_chunks, tk = v_t.shape[2], v_t.shape[4]
    return pl.pallas_call(
        functools.partial(_attn_kernel, tk=tk, n_chunks=n_chunks),
        grid=(B, N_KV_HEADS, S // tq),
        in_specs=[
            pl.BlockSpec((1, 1, Q_GROUP, HEAD_DIM, tq), lambda b, h, i: (b, h, 0, 0, i)),
            pl.BlockSpec((1, 1, S, HEAD_DIM), lambda b, h, i: (b, h, 0, 0)),
            pl.BlockSpec((1, 1, n_chunks, HEAD_DIM, tk), lambda b, h, i: (b, h, 0, 0, 0)),
            pl.BlockSpec((1, 1, 2, S), lambda b, h, i: (b, h, 0, 0)),
        ],
        out_specs=pl.BlockSpec((1, 1, Q_GROUP, HEAD_DIM, tq), lambda b, h, i: (b, h, 0, 0, i)),
        out_shape=jax.ShapeDtypeStruct((B, N_KV_HEADS, Q_GROUP, HEAD_DIM, S), BF16),
        scratch_shapes=[
            pltpu.VMEM((2, tk, Q_GROUP * tq), F32),
            pltpu.VMEM((2, 1, Q_GROUP * tq), F32),
            pltpu.VMEM((1, Q_GROUP * tq), F32),
            pltpu.VMEM((HEAD_DIM + ONES_ROWS, Q_GROUP * tq), F32),
        ],
        compiler_params=_params("parallel", "parallel", "parallel"),
        name="attn",
    )(q_t, k, v_t, k_norm2)


def _fft1_kernel(zr_ref, zi_ref, c1_ref, s1_ref, tc_ref, ts_ref, yr_out, yi_out):
    n1, sub, width = zr_ref.shape[1:]
    rows = n1 * sub
    zr = zr_ref[0].reshape(rows, width).astype(BF16)
    zi = zi_ref[0].reshape(rows, width).astype(BF16)
    c1, s1 = c1_ref[...], s1_ref[...]
    dot = functools.partial(jnp.dot, preferred_element_type=F32)
    yr = dot(c1, zr) + dot(s1, zi)
    yi = dot(c1, zi) - dot(s1, zr)
    tc = tc_ref[...].reshape(rows, width)
    ts = ts_ref[...].reshape(rows, width)
    yr_out[0] = (yr * tc + yi * ts).reshape(n1, sub, width)
    yi_out[0] = (yi * tc - yr * ts).reshape(n1, sub, width)


def _fft1_call(zr, zi, c1, s1, tc, ts):
    B, n1, n2, C = zr.shape
    sub = FFT1_SUBLANE_BLOCK
    blk = pl.BlockSpec((1, n1, sub, C), lambda b, i: (b, 0, i, 0))
    tw = pl.BlockSpec((n1, sub, C), lambda b, i: (0, i, 0))
    full = pl.BlockSpec((n1 * sub, n1 * sub), lambda b, i: (0, 0))
    return pl.pallas_call(
        _fft1_kernel,
        grid=(B, n2 // sub),
        in_specs=[blk, blk, full, full, tw, tw],
        out_specs=[blk, blk],
        out_shape=[jax.ShapeDtypeStruct((B, n1, n2, C), F32)] * 2,
        compiler_params=_params("parallel", "parallel"),
        name="fft_stage1",
    )(zr, zi, c1, s1, tc, ts)


def _fft2_kernel(yr_ref, yi_ref, c2_ref, s2_ref, o_ref, *, group):
    c2, s2 = c2_ref[...], s2_ref[...]
    width = yr_ref.shape[-1]
    for g in range(group):
        yr = yr_ref[0, g].astype(BF16)
        yi = yi_ref[0, g].astype(BF16)
        re = (jnp.dot(c2, yr, preferred_element_type=F32)
              + jnp.dot(s2, yi, preferred_element_type=F32))
        o_ref[0, :, g * width:(g + 1) * width] = re.astype(o_ref.dtype)


def _fft2_call(yr, yi, c2, s2, *, group):
    B, n1, n2, C = yr.shape
    blk = pl.BlockSpec((1, group, n2, C), lambda b, i: (b, i, 0, 0))
    full = pl.BlockSpec((n2, n2), lambda b, i: (0, 0))
    return pl.pallas_call(
        functools.partial(_fft2_kernel, group=group),
        grid=(B, n1 // group),
        in_specs=[blk, blk, full, full],
        out_specs=pl.BlockSpec((1, n2, group * C), lambda b, i: (b, 0, i)),
        out_shape=jax.ShapeDtypeStruct((B, n2, n1 * C), BF16),
        compiler_params=_params("parallel", "parallel"),
        name="fft_stage2",
    )(yr, yi, c2, s2)


def _merge_kernel(x_ref, at_ref, four_ref, gate_ref, wab_ref, wfb_ref, wout_ref, o_ref):
    D = x_ref.shape[-1]
    a = lax.dot_general(at_ref[0], wab_ref[...], (((0,), (0,)), ((), ())),
                        preferred_element_type=F32)
    f = jnp.dot(four_ref[0], wfb_ref[...], preferred_element_type=F32)
    gates = gate_ref[0].astype(F32)
    merged = (gates[:, :D] * a + gates[:, D:] * f).astype(BF16)
    o_ref[0] = x_ref[0] + jnp.dot(merged, wout_ref[...], preferred_element_type=F32)


def _merge_call(x, attn_t, four, gates, wab, wfb, wout, *, tm):
    B, S, D = x.shape
    full = lambda shape: pl.BlockSpec(shape, lambda b, i: (0,) * len(shape))
    row = lambda w: pl.BlockSpec((1, tm, w), lambda b, i: (b, i, 0))
    return pl.pallas_call(
        _merge_kernel,
        grid=(B, S // tm),
        in_specs=[
            row(D),
            pl.BlockSpec((1, Q_W, tm), lambda b, i: (b, 0, i)),
            row(F_W),
            row(2 * D),
            full((Q_W, D)), full((F_W, D)), full((D, D)),
        ],
        out_specs=row(D),
        out_shape=jax.ShapeDtypeStruct((B, S, D), F32),
        compiler_params=_params("parallel", "parallel"),
        name="merge",
    )(x, attn_t, four, gates, wab, wfb, wout)


def _mlp_kernel(x_ref, g_ref, wup_ref, wdown_ref, gf_ref, o_ref, *, ff_chunk, final_norm):
    x = x_ref[0]
    ms = jnp.mean(x * x, axis=-1, keepdims=True)
    h = (x * lax.rsqrt(ms + EPS) * g_ref[...]).astype(BF16)
    d_ff = wup_ref.shape[1]
    y = x
    for c in range(d_ff // ff_chunk):
        u = jnp.maximum(jnp.dot(h, wup_ref[:, c * ff_chunk:(c + 1) * ff_chunk],
                                preferred_element_type=F32), 0.0)
        y = y + jnp.dot((u * u).astype(BF16), wdown_ref[c * ff_chunk:(c + 1) * ff_chunk, :],
                        preferred_element_type=F32)
    if final_norm:
        ms = jnp.mean(y * y, axis=-1, keepdims=True)
        y = y * lax.rsqrt(ms + EPS) * gf_ref[...]
    o_ref[0] = y


def _mlp_call(x, g, wup, wdown, gf, *, tm, final_norm):
    B, S, D = x.shape
    d_ff = wup.shape[1]
    full = lambda shape: pl.BlockSpec(shape, lambda b, i: (0,) * len(shape))
    row = pl.BlockSpec((1, tm, D), lambda b, i: (b, i, 0))
    return pl.pallas_call(
        functools.partial(_mlp_kernel, ff_chunk=min(1024, d_ff), final_norm=final_norm),
        grid=(B, S // tm),
        in_specs=[row, full((1, D)), full((D, d_ff)), full((d_ff, D)), full((1, D))],
        out_specs=row,
        out_shape=jax.ShapeDtypeStruct((B, S, D), F32),
        compiler_params=_params("parallel", "parallel"),
        name="mlp",
    )(x, g, wup, wdown, gf)


def _rope_tables_t(seq):
    n_rows = seq // GRID_W
    rows = jnp.repeat(jnp.arange(n_rows, dtype=F32), GRID_W)
    cols = jnp.tile(jnp.arange(GRID_W, dtype=F32), n_rows)
    inv_freq = ROPE_THETA ** (-jnp.arange(ROPE_PAIRS_PER_AXIS, dtype=F32) / ROPE_PAIRS_PER_AXIS)
    ang = jnp.concatenate([rows[:, None] * inv_freq[None, :], cols[:, None] * inv_freq[None, :]], axis=-1)
    return jnp.cos(ang).T, jnp.sin(ang).T


def _dft_cos_sin(n, scale):
    idx = np.arange(n)
    ang = 2.0 * np.pi * ((idx[:, None] * idx[None, :]) % n) / n
    return np.cos(ang) * scale, np.sin(ang) * scale


def _fourier_constants(seq, n1, n2):
    gc, gs = _dft_cos_sin(FOURIER_GROUP_W, FOURIER_GROUP_W ** -0.5)
    eye = np.eye(N_FOURIER_GROUPS)
    cs = np.concatenate([np.kron(eye, gc), -np.kron(eye, gs)], axis=1)
    c1, s1 = (np.kron(m, np.eye(FFT1_SUBLANE_BLOCK)) for m in _dft_cos_sin(n1, n1 ** -0.5))
    c2, s2 = _dft_cos_sin(n2, n2 ** -0.5)
    k1 = np.arange(n1)[:, None]
    s2_idx = np.arange(n2)[None, :]
    ang = 2.0 * np.pi * ((k1 * s2_idx) % seq) / seq
    tc = np.repeat(np.cos(ang)[:, :, None], F_W, axis=2)
    ts = np.repeat(np.sin(ang)[:, :, None], F_W, axis=2)
    f32 = lambda a: jnp.asarray(a, dtype=F32)
    bf = lambda a: f32(a).astype(BF16)
    return bf(cs), bf(c1), bf(s1), bf(c2), bf(s2), f32(tc), f32(ts)


def _pair_split_perm():
    return np.concatenate([np.arange(0, HEAD_DIM, 2), np.arange(1, HEAD_DIM, 2)])


def kernel(x, norm_mix, w_in, b_gate, q_gain, k_gain, w_attn_branch, w_fourier_branch, w_out,
           norm_mlp, w_up, w_down, norm_final):
    B, S, D = x.shape
    depth = w_in.shape[0]
    tm, tk, tq, n1, n2 = _tiles(S)

    perm = _pair_split_perm()
    q_cols = (np.arange(N_HEADS)[:, None] * HEAD_DIM + perm[None, :]).reshape(-1)
    k_cols = Q_W + (np.arange(N_KV_HEADS)[:, None] * HEAD_DIM + perm[None, :]).reshape(-1)
    v_cols = Q_W + KV_W + np.arange(KV_W)
    qkv_cols = np.concatenate([q_cols, k_cols, v_cols])
    fg_start = Q_W + 2 * KV_W

    cos_t, sin_t = _rope_tables_t(S)
    cs, c1, s1, c2, s2, tc, ts = _fourier_constants(S, n1, n2)

    for l in range(depth):
        wqkv_t = w_in[l][:, qkv_cols].T.astype(BF16)
        wfg = w_in[l][:, fg_start:].astype(BF16)
        qg = q_gain[l][perm][:, None]
        kg = k_gain[l][perm][:, None]
        q_t, k, k_norm2, v_t, zr, zi, gates = _proj_call(
            x, norm_mix[l][None], wqkv_t, wfg, b_gate[l][None], qg, kg, cos_t, sin_t, cs, tm=tm, tk=tk)

        attn_t = _attn_call(q_t, k, v_t, k_norm2, tq=tq)
        attn_t = attn_t.reshape(B, Q_W, S)

        yr, yi = _fft1_call(zr.reshape(B, n1, n2, F_W), zi.reshape(B, n1, n2, F_W), c1, s1, tc, ts)
        four = _fft2_call(yr, yi, c2, s2, group=min(8, n1))
        four = four.reshape(B, S, F_W)

        x = _merge_call(x, attn_t, four, gates, w_attn_branch[l].astype(BF16),
                        w_fourier_branch[l].astype(BF16), w_out[l].astype(BF16), tm=tm)
        x = _mlp_call(x, norm_mlp[l][None], w_up[l].astype(BF16), w_down[l].astype(BF16),
                      norm_final[None], tm=tm, final_norm=(l == depth - 1))
    return x
```

```python
import functools
import math

import numpy as np
import jax
import jax.numpy as jnp
from jax import lax
from jax.experimental import pallas as pl
from jax.experimental.pallas import tpu as pltpu

HEAD_DIM = 64
N_HEADS = 12
N_KV_HEADS = 4
Q_GROUP = N_HEADS // N_KV_HEADS
FOURIER_GROUP_W = 64
N_FOURIER_GROUPS = 4
Q_W = N_HEADS * HEAD_DIM
KV_W = N_KV_HEADS * HEAD_DIM
F_W = N_FOURIER_GROUPS * FOURIER_GROUP_W
GRID_W = 64
ROPE_THETA = 10000.0
ROPE_PAIRS_PER_AXIS = HEAD_DIM // 4
EPS = 1e-6
HALF = HEAD_DIM // 2

V7X_VMEM_LIMIT_BYTES = 56 * 1024 * 1024
ONES_ROWS = 16
Q_SCALE = HEAD_DIM ** -0.5 * math.log2(math.e)
SAFE_SCORE_BOUND = 48.0
SAFE_VALUE_BOUND = 2.0 ** 40
BOUNDED_CHUNKS_PER_TRIP = 16
BOUNDED_CHUNKS_PER_PV = 1
BOUNDED_SCORE_LOOKAHEAD = 2
FFT1_SUBLANE_BLOCK = 8

BF16 = jnp.bfloat16
F32 = jnp.float32
F8 = jnp.float8_e4m3fn
QK8_DEPTH = 4 * HEAD_DIM
Q8_PRESCALE = 16.0
F8_SAFE_ABS = 400.0


def _tiles(seq):
    tm = min(512, seq)
    tk = min(256, seq)
    tq = min(512, seq)
    n1 = 64
    n2 = seq // n1
    assert seq % tm == 0 and tm % tk == 0 and seq % tq == 0 and n1 * n2 == seq
    assert (seq // tk) % 2 == 0
    return tm, tk, tq, n1, n2


def _params(*sem):
    return pltpu.CompilerParams(dimension_semantics=sem, vmem_limit_bytes=V7X_VMEM_LIMIT_BYTES)


def _proj_kernel(x_ref, g_ref, wqkv_ref, wfg_ref, bg_ref, qg_ref, kg_ref, cos_ref, sin_ref, cs_ref,
                 q_out, k_out, q8_out, k8_out, kn_out, v_out, zr_out, zi_out, gate_out, *, tk):
    x = x_ref[0]
    tm = x.shape[0]
    ms = jnp.mean(x * x, axis=-1, keepdims=True)
    h = (x * lax.rsqrt(ms + EPS) * g_ref[...]).astype(BF16)

    qkv_t = lax.dot_general(wqkv_ref[...], h, (((1,), (1,)), ((), ())),
                            preferred_element_type=F32)
    cos = cos_ref[...][None]
    sin = sin_ref[...][None]

    def norm_rope(t, gain, n_heads, scale):
        t = t.reshape(n_heads, HEAD_DIM, tm)
        msq = jnp.mean(t * t, axis=1, keepdims=True)
        y = t * lax.rsqrt(msq + EPS) * gain[None]
        a, b = y[:, :HALF], y[:, HALF:]
        out = jnp.concatenate([a * cos - b * sin, a * sin + b * cos], axis=1)
        return out * scale if scale != 1.0 else out

    q = norm_rope(qkv_t[:Q_W], qg_ref[...], N_HEADS, Q_SCALE)
    q_out[0] = q.reshape(N_KV_HEADS, Q_GROUP, HEAD_DIM, tm).astype(BF16)

    def split8(t):
        hi = t.astype(F8).astype(F32)
        return hi, (t - hi).astype(F8).astype(F32)

    q_hi, q_lo = split8(q * Q8_PRESCALE)
    zq = jnp.zeros_like(q_hi)
    q8 = jnp.concatenate([q_hi, q_hi, q_lo, zq], axis=1).astype(F8)
    q8_out[0] = q8.reshape(N_KV_HEADS, Q_GROUP, QK8_DEPTH, tm)

    k = norm_rope(qkv_t[Q_W:Q_W + KV_W], kg_ref[...], N_KV_HEADS, 1.0)
    k_used = k.astype(BF16).astype(F32)
    kn_out[0, :, 0:1] = jnp.sum(k_used * k_used, axis=1, keepdims=True)
    k_rows = k.reshape(KV_W, tm).T.astype(BF16)
    for hk in range(N_KV_HEADS):
        k_out[0, hk] = k_rows[:, hk * HEAD_DIM:(hk + 1) * HEAD_DIM]
    k_hi, k_lo = split8(k)
    k8_rows = jnp.concatenate([k_hi, k_lo, k_hi, jnp.zeros_like(k_hi)], axis=1)
    k8_rows = k8_rows.reshape(N_KV_HEADS * QK8_DEPTH, tm).T.astype(F8)
    for hk in range(N_KV_HEADS):
        k8_out[0, hk] = k8_rows[:, hk * QK8_DEPTH:(hk + 1) * QK8_DEPTH]

    v_t = qkv_t[Q_W + KV_W:].astype(BF16).reshape(N_KV_HEADS, HEAD_DIM, tm)
    kn_out[0, :, 1:2] = jnp.max(jnp.abs(v_t.astype(F32)), axis=1, keepdims=True)
    for c in range(tm // tk):
        v_out[0, :, c] = v_t[:, :, c * tk:(c + 1) * tk]

    fg = jnp.dot(h, wfg_ref[...], preferred_element_type=F32)
    z = jnp.dot(fg[:, :F_W].astype(BF16), cs_ref[...], preferred_element_type=F32)
    zr_out[0] = z[:, :F_W]
    zi_out[0] = z[:, F_W:]
    gate_out[0] = jax.nn.sigmoid(fg[:, F_W:] + bg_ref[...]).astype(BF16)


def _proj_call(x, g, wqkv_t, wfg, bg, qg, kg, cos_t, sin_t, cs, *, tm, tk):
    B, S, D = x.shape
    n_rows = wqkv_t.shape[0]
    gate_w = wfg.shape[1] - F_W
    full = lambda shape: pl.BlockSpec(shape, lambda b, i: (0,) * len(shape))
    return pl.pallas_call(
        functools.partial(_proj_kernel, tk=tk),
        grid=(B, S // tm),
        in_specs=[
            pl.BlockSpec((1, tm, D), lambda b, i: (b, i, 0)),
            full((1, D)),
            full((n_rows, D)),
            full((D, F_W + gate_w)),
            full((1, gate_w)),
            full((HEAD_DIM, 1)),
            full((HEAD_DIM, 1)),
            pl.BlockSpec((HALF, tm), lambda b, i: (0, i)),
            pl.BlockSpec((HALF, tm), lambda b, i: (0, i)),
            full((F_W, 2 * F_W)),
        ],
        out_specs=[
            pl.BlockSpec((1, N_KV_HEADS, Q_GROUP, HEAD_DIM, tm), lambda b, i: (b, 0, 0, 0, i)),
            pl.BlockSpec((1, N_KV_HEADS, tm, HEAD_DIM), lambda b, i: (b, 0, i, 0)),
            pl.BlockSpec((1, N_KV_HEADS, Q_GROUP, QK8_DEPTH, tm), lambda b, i: (b, 0, 0, 0, i)),
            pl.BlockSpec((1, N_KV_HEADS, tm, QK8_DEPTH), lambda b, i: (b, 0, i, 0)),
            pl.BlockSpec((1, N_KV_HEADS, 2, tm), lambda b, i: (b, 0, 0, i)),
            pl.BlockSpec((1, N_KV_HEADS, tm // tk, HEAD_DIM, tk), lambda b, i: (b, 0, i, 0, 0)),
            pl.BlockSpec((1, tm, F_W), lambda b, i: (b, i, 0)),
            pl.BlockSpec((1, tm, F_W), lambda b, i: (b, i, 0)),
            pl.BlockSpec((1, tm, gate_w), lambda b, i: (b, i, 0)),
        ],
        out_shape=[
            jax.ShapeDtypeStruct((B, N_KV_HEADS, Q_GROUP, HEAD_DIM, S), BF16),
            jax.ShapeDtypeStruct((B, N_KV_HEADS, S, HEAD_DIM), BF16),
            jax.ShapeDtypeStruct((B, N_KV_HEADS, Q_GROUP, QK8_DEPTH, S), F8),
            jax.ShapeDtypeStruct((B, N_KV_HEADS, S, QK8_DEPTH), F8),
            jax.ShapeDtypeStruct((B, N_KV_HEADS, 2, S), F32),
            jax.ShapeDtypeStruct((B, N_KV_HEADS, S // tk, HEAD_DIM, tk), BF16),
            jax.ShapeDtypeStruct((B, S, F_W), F32),
            jax.ShapeDtypeStruct((B, S, F_W), F32),
            jax.ShapeDtypeStruct((B, S, gate_w), BF16),
        ],
        compiler_params=_params("parallel", "parallel"),
        name="proj",
    )(x, g, wqkv_t, wfg, bg, qg, kg, cos_t, sin_t, cs)


def _attn_kernel(q_ref, k_ref, q8_ref, k8_ref, v_ref, kn_ref, o_ref, s_buf, cmax_ref, m_ref, acc_ref,
                 *, tk, n_chunks):
    tq = q_ref.shape[-1]
    q_cat = jnp.concatenate([q_ref[0, 0, g] for g in range(Q_GROUP)], axis=1)
    ones = jnp.ones((ONES_ROWS, tk), BF16)

    def scores(j, slot):
        start = pl.multiple_of(j * tk, tk)
        kc = k_ref[0, 0, pl.ds(start, tk), :]
        s_t = jnp.dot(kc, q_cat, preferred_element_type=F32)
        s_buf[slot] = s_t
        cmax_ref[slot] = jnp.max(s_t, axis=0, keepdims=True)

    def accumulate(j, slot):
        m_old = m_ref[...]
        m_new = jnp.maximum(m_old, cmax_ref[slot])
        alpha = jnp.exp2(m_old - m_new)
        p_t = jnp.exp2(s_buf[slot] - m_new).astype(BF16)
        v_ext = jnp.concatenate([v_ref[0, 0, j], ones], axis=0)
        acc_ref[...] = acc_ref[...] * alpha + jnp.dot(v_ext, p_t, preferred_element_type=F32)
        m_ref[...] = m_new

    def general_path():
        m_ref[...] = jnp.full(m_ref.shape, -jnp.inf, F32)
        scores(0, 0)

        def two_chunks(jj, carry):
            j0 = 2 * jj
            scores(j0 + 1, 1)
            accumulate(j0, 0)
            scores(jnp.minimum(j0 + 2, n_chunks - 1), 0)
            accumulate(j0 + 1, 1)
            return carry

        lax.fori_loop(0, n_chunks // 2, two_chunks, 0)

    def bounded_path():
        per_trip = math.gcd(BOUNDED_CHUNKS_PER_TRIP, n_chunks)
        q8_cat = jnp.concatenate([q8_ref[0, 0, g] for g in range(Q_GROUP)], axis=1)

        group = math.gcd(BOUNDED_CHUNKS_PER_PV, per_trip)
        n_groups = per_trip // group

        def trip(jj, carry):
            probs = {}

            def emit_scores(g):
                for c in range(g * group, (g + 1) * group):
                    start = pl.multiple_of((jj * per_trip + c) * tk, tk)
                    kc = k8_ref[0, 0, pl.ds(start, tk), :]
                    s_t = jnp.dot(kc, q8_cat, preferred_element_type=F32) * (1.0 / Q8_PRESCALE)
                    probs[c] = jnp.exp2(s_t).astype(BF16)

            def emit_pv(g):
                chunks = range(g * group, (g + 1) * group)
                p_t = jnp.concatenate([probs.pop(c) for c in chunks], axis=0)
                v_ext = jnp.concatenate(
                    [jnp.concatenate([v_ref[0, 0, jj * per_trip + c], ones], axis=0) for c in chunks],
                    axis=1)
                acc_ref[...] += jnp.dot(v_ext, p_t, preferred_element_type=F32)

            for g in range(min(BOUNDED_SCORE_LOOKAHEAD, n_groups)):
                emit_scores(g)
            for g in range(n_groups):
                emit_pv(g)
                if g + BOUNDED_SCORE_LOOKAHEAD < n_groups:
                    emit_scores(g + BOUNDED_SCORE_LOOKAHEAD)
            return carry

        lax.fori_loop(0, n_chunks // per_trip, trip, 0)

    acc_ref[...] = jnp.zeros(acc_ref.shape, F32)
    q32 = q_cat.astype(F32)
    q_norm2 = jnp.max(jnp.sum(q32 * q32, axis=0, keepdims=True))
    k_norm2 = jnp.max(kn_ref[0, 0, 0:1])
    v_max = jnp.max(kn_ref[0, 0, 1:2])
    bounded = jnp.logical_and(q_norm2 * k_norm2 <= SAFE_SCORE_BOUND ** 2, v_max <= SAFE_VALUE_BOUND)
    fits_f8 = jnp.logical_and(q_norm2 <= (F8_SAFE_ABS / Q8_PRESCALE) ** 2, k_norm2 <= F8_SAFE_ABS ** 2)
    lax.cond(jnp.logical_and(bounded, fits_f8), bounded_path, general_path)
    acc = acc_ref[...]
    out = acc[:HEAD_DIM] / acc[HEAD_DIM:HEAD_DIM + 1]
    for g in range(Q_GROUP):
        o_ref[0, 0, g] = out[:, g * tq:(g + 1) * tq].astype(o_ref.dtype)


def _attn_call(q_t, k, q8_t, k8, v_t, k_norm2, *, tq):
    B, _, _, _, S = q_t.shape
    n_chunks, tk = v_t.shape[2], v_t.shape[4]
    return pl.pallas_call(
        functools.partial(_attn_kernel, tk=tk, n_chunks=n_chunks),
        grid=(B, N_KV_HEADS, S // tq),
        in_specs=[
            pl.BlockSpec((1, 1, Q_GROUP, HEAD_DIM, tq), lambda b, h, i: (b, h, 0, 0, i)),
            pl.BlockSpec((1, 1, S, HEAD_DIM), lambda b, h, i: (b, h, 0, 0)),
            pl.BlockSpec((1, 1, Q_GROUP, QK8_DEPTH, tq), lambda b, h, i: (b, h, 0, 0, i)),
            pl.BlockSpec((1, 1, S, QK8_DEPTH), lambda b, h, i: (b, h, 0, 0)),
            pl.BlockSpec((1, 1, n_chunks, HEAD_DIM, tk), lambda b, h, i: (b, h, 0, 0, 0)),
            pl.BlockSpec((1, 1, 2, S), lambda b, h, i: (b, h, 0, 0)),
        ],
        out_specs=pl.BlockSpec((1, 1, Q_GROUP, HEAD_DIM, tq), lambda b, h, i: (b, h, 0, 0, i)),
        out_shape=jax.ShapeDtypeStruct((B, N_KV_HEADS, Q_GROUP, HEAD_DIM, S), BF16),
        scratch_shapes=[
            pltpu.VMEM((2, tk, Q_GROUP * tq), F32),
            pltpu.VMEM((2, 1, Q_GROUP * tq), F32),
            pltpu.VMEM((1, Q_GROUP * tq), F32),
            pltpu.VMEM((HEAD_DIM + ONES_ROWS, Q_GROUP * tq), F32),
        ],
        compiler_params=_params("parallel", "parallel", "parallel"),
        name="attn",
    )(q_t, k, q8_t, k8, v_t, k_norm2)


def _fft1_kernel(zr_ref, zi_ref, c1_ref, s1_ref, tc_ref, ts_ref, yr_out, yi_out):
    n1, sub, width = zr_ref.shape[1:]
    rows = n1 * sub
    zr = zr_ref[0].reshape(rows, width).astype(BF16)
    zi = zi_ref[0].reshape(rows, width).astype(BF16)
    c1, s1 = c1_ref[...], s1_ref[...]
    dot = functools.partial(jnp.dot, preferred_element_type=F32)
    yr = dot(c1, zr) + dot(s1, zi)
    yi = dot(c1, zi) - dot(s1, zr)
    tc = tc_ref[...].reshape(rows, width)
    ts = ts_ref[...].reshape(rows, width)
    yr_out[0] = (yr * tc + yi * ts).reshape(n1, sub, width)
    yi_out[0] = (yi * tc - yr * ts).reshape(n1, sub, width)


def _fft1_call(zr, zi, c1, s1, tc, ts):
    B, n1, n2, C = zr.shape
    sub = FFT1_SUBLANE_BLOCK
    blk = pl.BlockSpec((1, n1, sub, C), lambda i, b: (b, 0, i, 0))
    tw = pl.BlockSpec((n1, sub, C), lambda i, b: (0, i, 0))
    full = pl.BlockSpec((n1 * sub, n1 * sub), lambda i, b: (0, 0))
    return pl.pallas_call(
        _fft1_kernel,
        grid=(n2 // sub, B),
        in_specs=[blk, blk, full, full, tw, tw],
        out_specs=[blk, blk],
        out_shape=[jax.ShapeDtypeStruct((B, n1, n2, C), F32)] * 2,
        compiler_params=_params("parallel", "parallel"),
        name="fft_stage1",
    )(zr, zi, c1, s1, tc, ts)


def _fft2_kernel(yr_ref, yi_ref, c2_ref, s2_ref, o_ref, *, group):
    c2, s2 = c2_ref[...], s2_ref[...]
    width = yr_ref.shape[-1]
    for g in range(group):
        yr = yr_ref[0, g].astype(BF16)
        yi = yi_ref[0, g].astype(BF16)
        re = (jnp.dot(c2, yr, preferred_element_type=F32)
              + jnp.dot(s2, yi, preferred_element_type=F32))
        o_ref[0, :, g * width:(g + 1) * width] = re.astype(o_ref.dtype)


def _fft2_call(yr, yi, c2, s2, *, group):
    B, n1, n2, C = yr.shape
    blk = pl.BlockSpec((1, group, n2, C), lambda b, i: (b, i, 0, 0))
    full = pl.BlockSpec((n2, n2), lambda b, i: (0, 0))
    return pl.pallas_call(
        functools.partial(_fft2_kernel, group=group),
        grid=(B, n1 // group),
        in_specs=[blk, blk, full, full],
        out_specs=pl.BlockSpec((1, n2, group * C), lambda b, i: (b, 0, i)),
        out_shape=jax.ShapeDtypeStruct((B, n2, n1 * C), BF16),
        compiler_params=_params("parallel", "parallel"),
        name="fft_stage2",
    )(yr, yi, c2, s2)


def _merge_kernel(x_ref, at_ref, four_ref, gate_ref, wab_ref, wfb_ref, wout_ref, o_ref):
    D = x_ref.shape[-1]
    a = lax.dot_general(at_ref[0], wab_ref[...], (((0,), (0,)), ((), ())),
                        preferred_element_type=F32)
    f = jnp.dot(four_ref[0], wfb_ref[...], preferred_element_type=F32)
    gates = gate_ref[0].astype(F32)
    merged = (gates[:, :D] * a + gates[:, D:] * f).astype(BF16)
    o_ref[0] = x_ref[0] + jnp.dot(merged, wout_ref[...], preferred_element_type=F32)


def _merge_call(x, attn_t, four, gates, wab, wfb, wout, *, tm):
    B, S, D = x.shape
    full = lambda shape: pl.BlockSpec(shape, lambda b, i: (0,) * len(shape))
    row = lambda w: pl.BlockSpec((1, tm, w), lambda b, i: (b, i, 0))
    return pl.pallas_call(
        _merge_kernel,
        grid=(B, S // tm),
        in_specs=[
            row(D),
            pl.BlockSpec((1, Q_W, tm), lambda b, i: (b, 0, i)),
            row(F_W),
            row(2 * D),
            full((Q_W, D)), full((F_W, D)), full((D, D)),
        ],
        out_specs=row(D),
        out_shape=jax.ShapeDtypeStruct((B, S, D), F32),
        compiler_params=_params("parallel", "parallel"),
        name="merge",
    )(x, attn_t, four, gates, wab, wfb, wout)


def _mlp_kernel(x_ref, g_ref, wup_ref, wdown_ref, gf_ref, o_ref, *, ff_chunk, final_norm):
    x = x_ref[0]
    ms = jnp.mean(x * x, axis=-1, keepdims=True)
    h = (x * lax.rsqrt(ms + EPS) * g_ref[...]).astype(BF16)
    d_ff = wup_ref.shape[1]
    y = x
    for c in range(d_ff // ff_chunk):
        u = jnp.maximum(jnp.dot(h, wup_ref[:, c * ff_chunk:(c + 1) * ff_chunk],
                                preferred_element_type=F32), 0.0)
        y = y + jnp.dot((u * u).astype(BF16), wdown_ref[c * ff_chunk:(c + 1) * ff_chunk, :],
                        preferred_element_type=F32)
    if final_norm:
        ms = jnp.mean(y * y, axis=-1, keepdims=True)
        y = y * lax.rsqrt(ms + EPS) * gf_ref[...]
    o_ref[0] = y


def _mlp_call(x, g, wup, wdown, gf, *, tm, final_norm):
    B, S, D = x.shape
    d_ff = wup.shape[1]
    full = lambda shape: pl.BlockSpec(shape, lambda b, i: (0,) * len(shape))
    row = pl.BlockSpec((1, tm, D), lambda b, i: (b, i, 0))
    return pl.pallas_call(
        functools.partial(_mlp_kernel, ff_chunk=min(1024, d_ff), final_norm=final_norm),
        grid=(B, S // tm),
        in_specs=[row, full((1, D)), full((D, d_ff)), full((d_ff, D)), full((1, D))],
        out_specs=row,
        out_shape=jax.ShapeDtypeStruct((B, S, D), F32),
        compiler_params=_params("parallel", "parallel"),
        name="mlp",
    )(x, g, wup, wdown, gf)


def _rope_tables_t(seq):
    n_rows = seq // GRID_W
    rows = jnp.repeat(jnp.arange(n_rows, dtype=F32), GRID_W)
    cols = jnp.tile(jnp.arange(GRID_W, dtype=F32), n_rows)
    inv_freq = ROPE_THETA ** (-jnp.arange(ROPE_PAIRS_PER_AXIS, dtype=F32) / ROPE_PAIRS_PER_AXIS)
    ang = jnp.concatenate([rows[:, None] * inv_freq[None, :], cols[:, None] * inv_freq[None, :]], axis=-1)
    return jnp.cos(ang).T, jnp.sin(ang).T


def _dft_cos_sin(n, scale):
    idx = np.arange(n)
    ang = 2.0 * np.pi * ((idx[:, None] * idx[None, :]) % n) / n
    return np.cos(ang) * scale, np.sin(ang) * scale


def _fourier_constants(seq, n1, n2):
    gc, gs = _dft_cos_sin(FOURIER_GROUP_W, FOURIER_GROUP_W ** -0.5)
    eye = np.eye(N_FOURIER_GROUPS)
    cs = np.concatenate([np.kron(eye, gc), -np.kron(eye, gs)], axis=1)
    c1, s1 = (np.kron(m, np.eye(FFT1_SUBLANE_BLOCK)) for m in _dft_cos_sin(n1, n1 ** -0.5))
    c2, s2 = _dft_cos_sin(n2, n2 ** -0.5)
    k1 = np.arange(n1)[:, None]
    s2_idx = np.arange(n2)[None, :]
    ang = 2.0 * np.pi * ((k1 * s2_idx) % seq) / seq
    tc = np.repeat(np.cos(ang)[:, :, None], F_W, axis=2)
    ts = np.repeat(np.sin(ang)[:, :, None], F_W, axis=2)
    f32 = lambda a: jnp.asarray(a, dtype=F32)
    bf = lambda a: f32(a).astype(BF16)
    return bf(cs), bf(c1), bf(s1), bf(c2), bf(s2), f32(tc), f32(ts)


def _pair_split_perm():
    return np.concatenate([np.arange(0, HEAD_DIM, 2), np.arange(1, HEAD_DIM, 2)])


def kernel(x, norm_mix, w_in, b_gate, q_gain, k_gain, w_attn_branch, w_fourier_branch, w_out,
           norm_mlp, w_up, w_down, norm_final):
    B, S, D = x.shape
    depth = w_in.shape[0]
    tm, tk, tq, n1, n2 = _tiles(S)

    perm = _pair_split_perm()
    q_cols = (np.arange(N_HEADS)[:, None] * HEAD_DIM + perm[None, :]).reshape(-1)
    k_cols = Q_W + (np.arange(N_KV_HEADS)[:, None] * HEAD_DIM + perm[None, :]).reshape(-1)
    v_cols = Q_W + KV_W + np.arange(KV_W)
    qkv_cols = np.concatenate([q_cols, k_cols, v_cols])
    fg_start = Q_W + 2 * KV_W

    cos_t, sin_t = _rope_tables_t(S)
    cs, c1, s1, c2, s2, tc, ts = _fourier_constants(S, n1, n2)

    for l in range(depth):
        wqkv_t = w_in[l][:, qkv_cols].T.astype(BF16)
        wfg = w_in[l][:, fg_start:].astype(BF16)
        qg = q_gain[l][perm][:, None]
        kg = k_gain[l][perm][:, None]
        q_t, k, q8_t, k8, k_norm2, v_t, zr, zi, gates = _proj_call(
            x, norm_mix[l][None], wqkv_t, wfg, b_gate[l][None], qg, kg, cos_t, sin_t, cs, tm=tm, tk=tk)

        attn_t = _attn_call(q_t, k, q8_t, k8, v_t, k_norm2, tq=tq)
        attn_t = attn_t.reshape(B, Q_W, S)

        yr, yi = _fft1_call(zr.reshape(B, n1, n2, F_W), zi.reshape(B, n1, n2, F_W), c1, s1, tc, ts)
        four = _fft2_call(yr, yi, c2, s2, group=min(8, n1))
        four = four.reshape(B, S, F_W)

        x = _merge_call(x, attn_t, four, gates, w_attn_branch[l].astype(BF16),
                        w_fourier_branch[l].astype(BF16), w_out[l].astype(BF16), tm=tm)
        x = _mlp_call(x, norm_mlp[l][None], w_up[l].astype(BF16), w_down[l].astype(BF16),
                      norm_final[None], tm=tm, final_norm=(l == depth - 1))
    return x
```

```python
import functools
import math

import numpy as np
import jax
import jax.numpy as jnp
from jax import lax
from jax.experimental import pallas as pl
from jax.experimental.pallas import tpu as pltpu

HEAD_DIM = 64
N_HEADS = 12
N_KV_HEADS = 4
Q_GROUP = N_HEADS // N_KV_HEADS
FOURIER_GROUP_W = 64
N_FOURIER_GROUPS = 4
Q_W = N_HEADS * HEAD_DIM
KV_W = N_KV_HEADS * HEAD_DIM
F_W = N_FOURIER_GROUPS * FOURIER_GROUP_W
GRID_W = 64
ROPE_THETA = 10000.0
ROPE_PAIRS_PER_AXIS = HEAD_DIM // 4
EPS = 1e-6
HALF = HEAD_DIM // 2

V7X_VMEM_LIMIT_BYTES = 56 * 1024 * 1024
ONES_ROWS = 16
Q_SCALE = HEAD_DIM ** -0.5 * math.log2(math.e)
SAFE_SCORE_BOUND = 48.0
SAFE_VALUE_BOUND = 2.0 ** 40
BOUNDED_CHUNKS_PER_TRIP = 32
ATTN_LANE_TILE = 256
BOUNDED_SCORE_LOOKAHEAD = 4
FFT1_SUBLANE_BLOCK = 8
FFT1_POSITIONS_PER_STEP = 32

BF16 = jnp.bfloat16
F32 = jnp.float32
F8 = jnp.float8_e4m3fn
QK8_DEPTH = 4 * HEAD_DIM
Q8_PRESCALE = 16.0
F8_SAFE_ABS = 400.0


def _tiles(seq):
    tm = min(512, seq)
    tk = min(256, seq)
    tq = min(512, seq)
    n1 = 64
    n2 = seq // n1
    assert seq % tm == 0 and tm % tk == 0 and seq % tq == 0 and n1 * n2 == seq
    assert (seq // tk) % 2 == 0
    return tm, tk, tq, n1, n2


def _params(*sem):
    return pltpu.CompilerParams(dimension_semantics=sem, vmem_limit_bytes=V7X_VMEM_LIMIT_BYTES)


def _proj_kernel(x_ref, g_ref, wqkv_ref, wfg_ref, bg_ref, qg_ref, kg_ref, cos_ref, sin_ref, cs_ref,
                 q_out, k_out, q8_out, k8_out, kn_out, v_out, zr_out, zi_out, gate_out, *, tk):
    x = x_ref[0]
    tm = x.shape[0]
    ms = jnp.mean(x * x, axis=-1, keepdims=True)
    h = (x * lax.rsqrt(ms + EPS) * g_ref[...]).astype(BF16)

    qkv_t = lax.dot_general(wqkv_ref[...], h, (((1,), (1,)), ((), ())),
                            preferred_element_type=F32)
    cos = cos_ref[...][None]
    sin = sin_ref[...][None]

    def norm_rope(t, gain, n_heads, scale):
        t = t.reshape(n_heads, HEAD_DIM, tm)
        msq = jnp.mean(t * t, axis=1, keepdims=True)
        y = t * lax.rsqrt(msq + EPS) * gain[None]
        a, b = y[:, :HALF], y[:, HALF:]
        out = jnp.concatenate([a * cos - b * sin, a * sin + b * cos], axis=1)
        return out * scale if scale != 1.0 else out

    q = norm_rope(qkv_t[:Q_W], qg_ref[...], N_HEADS, Q_SCALE)
    q_out[0] = q.reshape(N_KV_HEADS, Q_GROUP, HEAD_DIM, tm).astype(BF16)

    def split8(t):
        hi = t.astype(F8).astype(F32)
        return hi, (t - hi).astype(F8).astype(F32)

    q_hi, q_lo = split8(q * Q8_PRESCALE)
    zq = jnp.zeros_like(q_hi)
    q8 = jnp.concatenate([q_hi, q_hi, q_lo, zq], axis=1).astype(F8)
    q8_out[0] = q8.reshape(N_KV_HEADS, Q_GROUP, QK8_DEPTH, tm)

    k = norm_rope(qkv_t[Q_W:Q_W + KV_W], kg_ref[...], N_KV_HEADS, 1.0)
    k_used = k.astype(BF16).astype(F32)
    kn_out[0, :, 0:1] = jnp.sum(k_used * k_used, axis=1, keepdims=True)
    k_rows = k.reshape(KV_W, tm).T.astype(BF16)
    for hk in range(N_KV_HEADS):
        k_out[0, hk] = k_rows[:, hk * HEAD_DIM:(hk + 1) * HEAD_DIM]
    k_hi, k_lo = split8(k)
    k8_rows = jnp.concatenate([k_hi, k_lo, k_hi, jnp.zeros_like(k_hi)], axis=1)
    k8_rows = k8_rows.reshape(N_KV_HEADS * QK8_DEPTH, tm).T.astype(F8)
    for hk in range(N_KV_HEADS):
        k8_out[0, hk] = k8_rows[:, hk * QK8_DEPTH:(hk + 1) * QK8_DEPTH]

    v_t = qkv_t[Q_W + KV_W:].astype(BF16).reshape(N_KV_HEADS, HEAD_DIM, tm)
    kn_out[0, :, 1:2] = jnp.max(jnp.abs(v_t.astype(F32)), axis=1, keepdims=True)
    for c in range(tm // tk):
        v_out[0, :, c] = v_t[:, :, c * tk:(c + 1) * tk]

    fg = jnp.dot(h, wfg_ref[...], preferred_element_type=F32)
    z = jnp.dot(fg[:, :F_W].astype(BF16), cs_ref[...], preferred_element_type=F32)
    zr_out[0] = z[:, :F_W]
    zi_out[0] = z[:, F_W:]
    gate_out[0] = jax.nn.sigmoid(fg[:, F_W:] + bg_ref[...]).astype(BF16)


def _proj_call(x, g, wqkv_t, wfg, bg, qg, kg, cos_t, sin_t, cs, *, tm, tk):
    B, S, D = x.shape
    n_rows = wqkv_t.shape[0]
    gate_w = wfg.shape[1] - F_W
    full = lambda shape: pl.BlockSpec(shape, lambda b, i: (0,) * len(shape))
    return pl.pallas_call(
        functools.partial(_proj_kernel, tk=tk),
        grid=(B, S // tm),
        in_specs=[
            pl.BlockSpec((1, tm, D), lambda b, i: (b, i, 0)),
            full((1, D)),
            full((n_rows, D)),
            full((D, F_W + gate_w)),
            full((1, gate_w)),
            full((HEAD_DIM, 1)),
            full((HEAD_DIM, 1)),
            pl.BlockSpec((HALF, tm), lambda b, i: (0, i)),
            pl.BlockSpec((HALF, tm), lambda b, i: (0, i)),
            full((F_W, 2 * F_W)),
        ],
        out_specs=[
            pl.BlockSpec((1, N_KV_HEADS, Q_GROUP, HEAD_DIM, tm), lambda b, i: (b, 0, 0, 0, i)),
            pl.BlockSpec((1, N_KV_HEADS, tm, HEAD_DIM), lambda b, i: (b, 0, i, 0)),
            pl.BlockSpec((1, N_KV_HEADS, Q_GROUP, QK8_DEPTH, tm), lambda b, i: (b, 0, 0, 0, i)),
            pl.BlockSpec((1, N_KV_HEADS, tm, QK8_DEPTH), lambda b, i: (b, 0, i, 0)),
            pl.BlockSpec((1, N_KV_HEADS, 2, tm), lambda b, i: (b, 0, 0, i)),
            pl.BlockSpec((1, N_KV_HEADS, tm // tk, HEAD_DIM, tk), lambda b, i: (b, 0, i, 0, 0)),
            pl.BlockSpec((1, tm, F_W), lambda b, i: (b, i, 0)),
            pl.BlockSpec((1, tm, F_W), lambda b, i: (b, i, 0)),
            pl.BlockSpec((1, tm, gate_w), lambda b, i: (b, i, 0)),
        ],
        out_shape=[
            jax.ShapeDtypeStruct((B, N_KV_HEADS, Q_GROUP, HEAD_DIM, S), BF16),
            jax.ShapeDtypeStruct((B, N_KV_HEADS, S, HEAD_DIM), BF16),
            jax.ShapeDtypeStruct((B, N_KV_HEADS, Q_GROUP, QK8_DEPTH, S), F8),
            jax.ShapeDtypeStruct((B, N_KV_HEADS, S, QK8_DEPTH), F8),
            jax.ShapeDtypeStruct((B, N_KV_HEADS, 2, S), F32),
            jax.ShapeDtypeStruct((B, N_KV_HEADS, S // tk, HEAD_DIM, tk), BF16),
            jax.ShapeDtypeStruct((B, S, F_W), F32),
            jax.ShapeDtypeStruct((B, S, F_W), F32),
            jax.ShapeDtypeStruct((B, S, gate_w), BF16),
        ],
        compiler_params=_params("parallel", "parallel"),
        name="proj",
    )(x, g, wqkv_t, wfg, bg, qg, kg, cos_t, sin_t, cs)


def _attn_kernel(q_ref, k_ref, q8_ref, k8_ref, v_ref, kn_ref, o_ref, s_buf, cmax_ref, m_ref, acc_ref,
                 *, tk, n_chunks):
    tq = q_ref.shape[-1]
    q_cat = jnp.concatenate([q_ref[0, 0, g] for g in range(Q_GROUP)], axis=1)
    ones = jnp.ones((ONES_ROWS, tk), BF16)

    def scores(j, slot):
        start = pl.multiple_of(j * tk, tk)
        kc = k_ref[0, 0, pl.ds(start, tk), :]
        s_t = jnp.dot(kc, q_cat, preferred_element_type=F32)
        s_buf[slot] = s_t
        cmax_ref[slot] = jnp.max(s_t, axis=0, keepdims=True)

    def accumulate(j, slot):
        m_old = m_ref[...]
        m_new = jnp.maximum(m_old, cmax_ref[slot])
        alpha = jnp.exp2(m_old - m_new)
        p_t = jnp.exp2(s_buf[slot] - m_new).astype(BF16)
        v_ext = jnp.concatenate([v_ref[0, 0, j], ones], axis=0)
        acc_ref[...] = acc_ref[...] * alpha + jnp.dot(v_ext, p_t, preferred_element_type=F32)
        m_ref[...] = m_new

    def general_path():
        m_ref[...] = jnp.full(m_ref.shape, -jnp.inf, F32)
        scores(0, 0)

        def two_chunks(jj, carry):
            j0 = 2 * jj
            scores(j0 + 1, 1)
            accumulate(j0, 0)
            scores(jnp.minimum(j0 + 2, n_chunks - 1), 0)
            accumulate(j0 + 1, 1)
            return carry

        lax.fori_loop(0, n_chunks // 2, two_chunks, 0)

    def bounded_path():
        per_trip = math.gcd(BOUNDED_CHUNKS_PER_TRIP, n_chunks)
        q8_cat = jnp.concatenate([q8_ref[0, 0, g] for g in range(Q_GROUP)], axis=1)

        n_lane_tiles = q8_cat.shape[1] // ATTN_LANE_TILE
        n_tiles = per_trip * n_lane_tiles

        def trip(jj, carry):
            probs = {}

            def emit_scores(t):
                c, n = divmod(t, n_lane_tiles)
                start = pl.multiple_of((jj * per_trip + c) * tk, tk)
                kc = k8_ref[0, 0, pl.ds(start, tk), :]
                q_tile = q8_cat[:, n * ATTN_LANE_TILE:(n + 1) * ATTN_LANE_TILE]
                s_t = jnp.dot(kc, q_tile, preferred_element_type=F32) * (1.0 / Q8_PRESCALE)
                probs[t] = jnp.exp2(s_t).astype(BF16)

            def emit_pv(t):
                c, n = divmod(t, n_lane_tiles)
                v_ext = jnp.concatenate([v_ref[0, 0, jj * per_trip + c], ones], axis=0)
                lanes = slice(n * ATTN_LANE_TILE, (n + 1) * ATTN_LANE_TILE)
                acc_ref[:, lanes] += jnp.dot(v_ext, probs.pop(t), preferred_element_type=F32)

            for t in range(min(BOUNDED_SCORE_LOOKAHEAD, n_tiles)):
                emit_scores(t)
            for t in range(n_tiles):
                emit_pv(t)
                if t + BOUNDED_SCORE_LOOKAHEAD < n_tiles:
                    emit_scores(t + BOUNDED_SCORE_LOOKAHEAD)
            return carry

        lax.fori_loop(0, n_chunks // per_trip, trip, 0)

    acc_ref[...] = jnp.zeros(acc_ref.shape, F32)
    q32 = q_cat.astype(F32)
    q_norm2 = jnp.max(jnp.sum(q32 * q32, axis=0, keepdims=True))
    k_norm2 = jnp.max(kn_ref[0, 0, 0:1])
    v_max = jnp.max(kn_ref[0, 0, 1:2])
    bounded = jnp.logical_and(q_norm2 * k_norm2 <= SAFE_SCORE_BOUND ** 2, v_max <= SAFE_VALUE_BOUND)
    fits_f8 = jnp.logical_and(q_norm2 <= (F8_SAFE_ABS / Q8_PRESCALE) ** 2, k_norm2 <= F8_SAFE_ABS ** 2)
    lax.cond(jnp.logical_and(bounded, fits_f8), bounded_path, general_path)
    acc = acc_ref[...]
    out = acc[:HEAD_DIM] / acc[HEAD_DIM:HEAD_DIM + 1]
    for g in range(Q_GROUP):
        o_ref[0, 0, g] = out[:, g * tq:(g + 1) * tq].astype(o_ref.dtype)


def _attn_call(q_t, k, q8_t, k8, v_t, k_norm2, *, tq):
    B, _, _, _, S = q_t.shape
    n_chunks, tk = v_t.shape[2], v_t.shape[4]
    return pl.pallas_call(
        functools.partial(_attn_kernel, tk=tk, n_chunks=n_chunks),
        grid=(B, N_KV_HEADS, S // tq),
        in_specs=[
            pl.BlockSpec((1, 1, Q_GROUP, HEAD_DIM, tq), lambda b, h, i: (b, h, 0, 0, i)),
            pl.BlockSpec((1, 1, S, HEAD_DIM), lambda b, h, i: (b, h, 0, 0)),
            pl.BlockSpec((1, 1, Q_GROUP, QK8_DEPTH, tq), lambda b, h, i: (b, h, 0, 0, i)),
            pl.BlockSpec((1, 1, S, QK8_DEPTH), lambda b, h, i: (b, h, 0, 0)),
            pl.BlockSpec((1, 1, n_chunks, HEAD_DIM, tk), lambda b, h, i: (b, h, 0, 0, 0)),
            pl.BlockSpec((1, 1, 2, S), lambda b, h, i: (b, h, 0, 0)),
        ],
        out_specs=pl.BlockSpec((1, 1, Q_GROUP, HEAD_DIM, tq), lambda b, h, i: (b, h, 0, 0, i)),
        out_shape=jax.ShapeDtypeStruct((B, N_KV_HEADS, Q_GROUP, HEAD_DIM, S), BF16),
        scratch_shapes=[
            pltpu.VMEM((2, tk, Q_GROUP * tq), F32),
            pltpu.VMEM((2, 1, Q_GROUP * tq), F32),
            pltpu.VMEM((1, Q_GROUP * tq), F32),
            pltpu.VMEM((HEAD_DIM + ONES_ROWS, Q_GROUP * tq), F32),
        ],
        compiler_params=_params("parallel", "parallel", "parallel"),
        name="attn",
    )(q_t, k, q8_t, k8, v_t, k_norm2)


def _fft1_kernel(zr_ref, zi_ref, c1_ref, s1_ref, tc_ref, ts_ref, yr_out, yi_out):
    n1, span, width = zr_ref.shape[1:]
    sub = FFT1_SUBLANE_BLOCK
    rows = n1 * sub
    c1, s1 = c1_ref[...], s1_ref[...]
    dot = functools.partial(jnp.dot, preferred_element_type=F32)
    for t in range(span // sub):
        part = slice(t * sub, (t + 1) * sub)
        zr = zr_ref[0, :, part, :].reshape(rows, width).astype(BF16)
        zi = zi_ref[0, :, part, :].reshape(rows, width).astype(BF16)
        yr = dot(c1, zr) + dot(s1, zi)
        yi = dot(c1, zi) - dot(s1, zr)
        tc = tc_ref[:, part, :].reshape(rows, width)
        ts = ts_ref[:, part, :].reshape(rows, width)
        yr_out[0, :, part, :] = (yr * tc + yi * ts).reshape(n1, sub, width)
        yi_out[0, :, part, :] = (yi * tc - yr * ts).reshape(n1, sub, width)


def _fft1_call(zr, zi, c1, s1, tc, ts):
    B, n1, n2, C = zr.shape
    sub = math.gcd(FFT1_POSITIONS_PER_STEP, n2)
    blk = pl.BlockSpec((1, n1, sub, C), lambda i, b: (b, 0, i, 0))
    tw = pl.BlockSpec((n1, sub, C), lambda i, b: (0, i, 0))
    full = pl.BlockSpec(c1.shape, lambda i, b: (0, 0))
    return pl.pallas_call(
        _fft1_kernel,
        grid=(n2 // sub, B),
        in_specs=[blk, blk, full, full, tw, tw],
        out_specs=[blk, blk],
        out_shape=[jax.ShapeDtypeStruct((B, n1, n2, C), F32)] * 2,
        compiler_params=_params("parallel", "parallel"),
        name="fft_stage1",
    )(zr, zi, c1, s1, tc, ts)


def _fft2_kernel(yr_ref, yi_ref, c2_ref, s2_ref, o_ref, *, group):
    c2, s2 = c2_ref[...], s2_ref[...]
    width = yr_ref.shape[-1]
    for g in range(group):
        yr = yr_ref[0, g].astype(BF16)
        yi = yi_ref[0, g].astype(BF16)
        re = (jnp.dot(c2, yr, preferred_element_type=F32)
              + jnp.dot(s2, yi, preferred_element_type=F32))
        o_ref[0, :, g, :] = re


def _fft2_call(yr, yi, c2, s2, *, group):
    B, n1, n2, C = yr.shape
    blk = pl.BlockSpec((1, group, n2, C), lambda b, i: (b, i, 0, 0))
    full = pl.BlockSpec((n2, n2), lambda b, i: (0, 0))
    return pl.pallas_call(
        functools.partial(_fft2_kernel, group=group),
        grid=(B, n1 // group),
        in_specs=[blk, blk, full, full],
        out_specs=pl.BlockSpec((1, n2, group, C), lambda b, i: (b, 0, i, 0)),
        out_shape=jax.ShapeDtypeStruct((B, n2, n1, C), F32),
        compiler_params=_params("parallel", "parallel"),
        name="fft_stage2",
    )(yr, yi, c2, s2)


def _merge_kernel(x_ref, at_ref, four_ref, gate_ref, wab_ref, wfb_ref, wout_ref, o_ref):
    D = x_ref.shape[-1]
    a = lax.dot_general(at_ref[0], wab_ref[...], (((0,), (0,)), ((), ())),
                        preferred_element_type=F32)
    f = jnp.dot(four_ref[0].astype(BF16), wfb_ref[...], preferred_element_type=F32)
    gates = gate_ref[0].astype(F32)
    merged = (gates[:, :D] * a + gates[:, D:] * f).astype(BF16)
    o_ref[0] = x_ref[0] + jnp.dot(merged, wout_ref[...], preferred_element_type=F32)


def _merge_call(x, attn_t, four, gates, wab, wfb, wout, *, tm):
    B, S, D = x.shape
    full = lambda shape: pl.BlockSpec(shape, lambda b, i: (0,) * len(shape))
    row = lambda w: pl.BlockSpec((1, tm, w), lambda b, i: (b, i, 0))
    return pl.pallas_call(
        _merge_kernel,
        grid=(B, S // tm),
        in_specs=[
            row(D),
            pl.BlockSpec((1, Q_W, tm), lambda b, i: (b, 0, i)),
            row(F_W),
            row(2 * D),
            full((Q_W, D)), full((F_W, D)), full((D, D)),
        ],
        out_specs=row(D),
        out_shape=jax.ShapeDtypeStruct((B, S, D), F32),
        compiler_params=_params("parallel", "parallel"),
        name="merge",
    )(x, attn_t, four, gates, wab, wfb, wout)


def _mlp_kernel(x_ref, g_ref, wup_ref, wdown_ref, gf_ref, o_ref, *, ff_chunk, final_norm):
    x = x_ref[0]
    ms = jnp.mean(x * x, axis=-1, keepdims=True)
    h = (x * lax.rsqrt(ms + EPS) * g_ref[...]).astype(BF16)
    d_ff = wup_ref.shape[1]
    y = x
    for c in range(d_ff // ff_chunk):
        u = jnp.maximum(jnp.dot(h, wup_ref[:, c * ff_chunk:(c + 1) * ff_chunk],
                                preferred_element_type=F32), 0.0)
        y = y + jnp.dot((u * u).astype(BF16), wdown_ref[c * ff_chunk:(c + 1) * ff_chunk, :],
                        preferred_element_type=F32)
    if final_norm:
        ms = jnp.mean(y * y, axis=-1, keepdims=True)
        y = y * lax.rsqrt(ms + EPS) * gf_ref[...]
    o_ref[0] = y


def _mlp_call(x, g, wup, wdown, gf, *, tm, final_norm):
    B, S, D = x.shape
    d_ff = wup.shape[1]
    full = lambda shape: pl.BlockSpec(shape, lambda b, i: (0,) * len(shape))
    row = pl.BlockSpec((1, tm, D), lambda b, i: (b, i, 0))
    return pl.pallas_call(
        functools.partial(_mlp_kernel, ff_chunk=min(1024, d_ff), final_norm=final_norm),
        grid=(B, S // tm),
        in_specs=[row, full((1, D)), full((D, d_ff)), full((d_ff, D)), full((1, D))],
        out_specs=row,
        out_shape=jax.ShapeDtypeStruct((B, S, D), F32),
        compiler_params=_params("parallel", "parallel"),
        name="mlp",
    )(x, g, wup, wdown, gf)


def _rope_tables_t(seq):
    n_rows = seq // GRID_W
    rows = jnp.repeat(jnp.arange(n_rows, dtype=F32), GRID_W)
    cols = jnp.tile(jnp.arange(GRID_W, dtype=F32), n_rows)
    inv_freq = ROPE_THETA ** (-jnp.arange(ROPE_PAIRS_PER_AXIS, dtype=F32) / ROPE_PAIRS_PER_AXIS)
    ang = jnp.concatenate([rows[:, None] * inv_freq[None, :], cols[:, None] * inv_freq[None, :]], axis=-1)
    return jnp.cos(ang).T, jnp.sin(ang).T


def _dft_cos_sin(n, scale):
    idx = np.arange(n)
    ang = 2.0 * np.pi * ((idx[:, None] * idx[None, :]) % n) / n
    return np.cos(ang) * scale, np.sin(ang) * scale


def _fourier_constants(seq, n1, n2):
    gc, gs = _dft_cos_sin(FOURIER_GROUP_W, FOURIER_GROUP_W ** -0.5)
    eye = np.eye(N_FOURIER_GROUPS)
    cs = np.concatenate([np.kron(eye, gc), -np.kron(eye, gs)], axis=1)
    c1, s1 = (np.kron(m, np.eye(FFT1_SUBLANE_BLOCK)) for m in _dft_cos_sin(n1, n1 ** -0.5))
    c2, s2 = _dft_cos_sin(n2, n2 ** -0.5)
    k1 = np.arange(n1)[:, None]
    s2_idx = np.arange(n2)[None, :]
    ang = 2.0 * np.pi * ((k1 * s2_idx) % seq) / seq
    tc = np.repeat(np.cos(ang)[:, :, None], F_W, axis=2)
    ts = np.repeat(np.sin(ang)[:, :, None], F_W, axis=2)
    f32 = lambda a: jnp.asarray(a, dtype=F32)
    bf = lambda a: f32(a).astype(BF16)
    return bf(cs), bf(c1), bf(s1), bf(c2), bf(s2), f32(tc), f32(ts)


def _pair_split_perm():
    return np.concatenate([np.arange(0, HEAD_DIM, 2), np.arange(1, HEAD_DIM, 2)])


def kernel(x, norm_mix, w_in, b_gate, q_gain, k_gain, w_attn_branch, w_fourier_branch, w_out,
           norm_mlp, w_up, w_down, norm_final):
    B, S, D = x.shape
    depth = w_in.shape[0]
    tm, tk, tq, n1, n2 = _tiles(S)

    perm = _pair_split_perm()
    q_cols = (np.arange(N_HEADS)[:, None] * HEAD_DIM + perm[None, :]).reshape(-1)
    k_cols = Q_W + (np.arange(N_KV_HEADS)[:, None] * HEAD_DIM + perm[None, :]).reshape(-1)
    v_cols = Q_W + KV_W + np.arange(KV_W)
    qkv_cols = np.concatenate([q_cols, k_cols, v_cols])
    fg_start = Q_W + 2 * KV_W

    cos_t, sin_t = _rope_tables_t(S)
    cs, c1, s1, c2, s2, tc, ts = _fourier_constants(S, n1, n2)

    for l in range(depth):
        wqkv_t = w_in[l][:, qkv_cols].T.astype(BF16)
        wfg = w_in[l][:, fg_start:].astype(BF16)
        qg = q_gain[l][perm][:, None]
        kg = k_gain[l][perm][:, None]
        q_t, k, q8_t, k8, k_norm2, v_t, zr, zi, gates = _proj_call(
            x, norm_mix[l][None], wqkv_t, wfg, b_gate[l][None], qg, kg, cos_t, sin_t, cs, tm=tm, tk=tk)

        attn_t = _attn_call(q_t, k, q8_t, k8, v_t, k_norm2, tq=tq)
        attn_t = attn_t.reshape(B, Q_W, S)

        yr, yi = _fft1_call(zr.reshape(B, n1, n2, F_W), zi.reshape(B, n1, n2, F_W), c1, s1, tc, ts)
        four = _fft2_call(yr, yi, c2, s2, group=min(8, n1))
        four = four.reshape(B, S, F_W)

        x = _merge_call(x, attn_t, four, gates, w_attn_branch[l].astype(BF16),
                        w_fourier_branch[l].astype(BF16), w_out[l].astype(BF16), tm=tm)
        x = _mlp_call(x, norm_mlp[l][None], w_up[l].astype(BF16), w_down[l].astype(BF16),
                      norm_final[None], tm=tm, final_norm=(l == depth - 1))
    return x
```

```python
import functools
import math

import numpy as np
import jax
import jax.numpy as jnp
from jax import lax
from jax.experimental import pallas as pl
from jax.experimental.pallas import tpu as pltpu

HEAD_DIM = 64
N_HEADS = 12
N_KV_HEADS = 4
Q_GROUP = N_HEADS // N_KV_HEADS
FOURIER_GROUP_W = 64
N_FOURIER_GROUPS = 4
Q_W = N_HEADS * HEAD_DIM
KV_W = N_KV_HEADS * HEAD_DIM
F_W = N_FOURIER_GROUPS * FOURIER_GROUP_W
GRID_W = 64
ROPE_THETA = 10000.0
ROPE_PAIRS_PER_AXIS = HEAD_DIM // 4
EPS = 1e-6
HALF = HEAD_DIM // 2

V7X_VMEM_LIMIT_BYTES = 56 * 1024 * 1024
ONES_ROWS = 16
Q_SCALE = HEAD_DIM ** -0.5 * math.log2(math.e)
SAFE_SCORE_BOUND = 48.0
SAFE_VALUE_BOUND = 2.0 ** 40
BOUNDED_CHUNKS_PER_TRIP = 32
ATTN_LANE_TILE = 256
BOUNDED_SCORE_LOOKAHEAD = 4
FFT1_SUBLANE_BLOCK = 8
FFT1_POSITIONS_PER_STEP = 32

BF16 = jnp.bfloat16
F32 = jnp.float32
F8 = jnp.float8_e4m3fn
QK8_DEPTH = 4 * HEAD_DIM
Q8_PRESCALE = 16.0
F8_SAFE_ABS = 400.0


def _tiles(seq):
    tm = min(512, seq)
    tp = min(1024, seq)
    tk = min(256, seq)
    tq = min(1024, seq)
    n1 = 64
    n2 = seq // n1
    assert seq % tm == 0 and seq % tp == 0 and tp % tk == 0 and seq % tq == 0 and n1 * n2 == seq
    assert (seq // tk) % 2 == 0
    return tm, tp, tk, tq, n1, n2


def _params(*sem):
    return pltpu.CompilerParams(dimension_semantics=sem, vmem_limit_bytes=V7X_VMEM_LIMIT_BYTES)


def _proj_kernel(x_ref, g_ref, wqkv_ref, wf_ref, qg_ref, kg_ref, cos_ref, sin_ref, cs_ref,
                 q_out, k_out, q8_out, k8_out, kn_out, v_out, zr_out, zi_out, *, tk):
    x = x_ref[0]
    tm = x.shape[0]
    ms = jnp.mean(x * x, axis=-1, keepdims=True)
    h = (x * lax.rsqrt(ms + EPS) * g_ref[...]).astype(BF16)

    qkv_t = lax.dot_general(wqkv_ref[...], h, (((1,), (1,)), ((), ())),
                            preferred_element_type=F32)
    cos = cos_ref[...][None]
    sin = sin_ref[...][None]

    def norm_rope(t, gain, n_heads, scale):
        t = t.reshape(n_heads, HEAD_DIM, tm)
        msq = jnp.mean(t * t, axis=1, keepdims=True)
        y = t * lax.rsqrt(msq + EPS) * gain[None]
        a, b = y[:, :HALF], y[:, HALF:]
        out = jnp.concatenate([a * cos - b * sin, a * sin + b * cos], axis=1)
        return out * scale if scale != 1.0 else out

    q = norm_rope(qkv_t[:Q_W], qg_ref[...], N_HEADS, Q_SCALE)
    q_out[0] = q.reshape(N_KV_HEADS, Q_GROUP, HEAD_DIM, tm).astype(BF16)

    def split8(t):
        hi = t.astype(F8).astype(F32)
        return hi, (t - hi).astype(F8).astype(F32)

    q_hi, q_lo = split8(q * Q8_PRESCALE)
    zq = jnp.zeros_like(q_hi)
    q8 = jnp.concatenate([q_hi, q_hi, q_lo, zq], axis=1).astype(F8)
    q8_out[0] = q8.reshape(N_KV_HEADS, Q_GROUP, QK8_DEPTH, tm)

    k = norm_rope(qkv_t[Q_W:Q_W + KV_W], kg_ref[...], N_KV_HEADS, 1.0)
    k_used = k.astype(BF16).astype(F32)
    kn_out[0, :, 0:1] = jnp.sum(k_used * k_used, axis=1, keepdims=True)
    k_rows = k.reshape(KV_W, tm).T.astype(BF16)
    for hk in range(N_KV_HEADS):
        k_out[0, hk] = k_rows[:, hk * HEAD_DIM:(hk + 1) * HEAD_DIM]
    k_hi, k_lo = split8(k)
    k8_rows = jnp.concatenate([k_hi, k_lo, k_hi, jnp.zeros_like(k_hi)], axis=1)
    k8_rows = k8_rows.reshape(N_KV_HEADS * QK8_DEPTH, tm).T.astype(F8)
    for hk in range(N_KV_HEADS):
        k8_out[0, hk] = k8_rows[:, hk * QK8_DEPTH:(hk + 1) * QK8_DEPTH]

    v_t = qkv_t[Q_W + KV_W:].astype(BF16).reshape(N_KV_HEADS, HEAD_DIM, tm)
    kn_out[0, :, 1:2] = jnp.max(jnp.abs(v_t.astype(F32)), axis=1, keepdims=True)
    for c in range(tm // tk):
        v_out[0, :, c] = v_t[:, :, c * tk:(c + 1) * tk]

    f = jnp.dot(h, wf_ref[...], preferred_element_type=F32)
    z = jnp.dot(f.astype(BF16), cs_ref[...], preferred_element_type=F32)
    zr_out[0] = z[:, :F_W]
    zi_out[0] = z[:, F_W:]


def _proj_call(x, g, wqkv_t, wf, qg, kg, cos_t, sin_t, cs, *, tm, tk):
    B, S, D = x.shape
    n_rows = wqkv_t.shape[0]
    full = lambda shape: pl.BlockSpec(shape, lambda b, i: (0,) * len(shape))
    return pl.pallas_call(
        functools.partial(_proj_kernel, tk=tk),
        grid=(B, S // tm),
        in_specs=[
            pl.BlockSpec((1, tm, D), lambda b, i: (b, i, 0)),
            full((1, D)),
            full((n_rows, D)),
            full((D, F_W)),
            full((HEAD_DIM, 1)),
            full((HEAD_DIM, 1)),
            pl.BlockSpec((HALF, tm), lambda b, i: (0, i)),
            pl.BlockSpec((HALF, tm), lambda b, i: (0, i)),
            full((F_W, 2 * F_W)),
        ],
        out_specs=[
            pl.BlockSpec((1, N_KV_HEADS, Q_GROUP, HEAD_DIM, tm), lambda b, i: (b, 0, 0, 0, i)),
            pl.BlockSpec((1, N_KV_HEADS, tm, HEAD_DIM), lambda b, i: (b, 0, i, 0)),
            pl.BlockSpec((1, N_KV_HEADS, Q_GROUP, QK8_DEPTH, tm), lambda b, i: (b, 0, 0, 0, i)),
            pl.BlockSpec((1, N_KV_HEADS, tm, QK8_DEPTH), lambda b, i: (b, 0, i, 0)),
            pl.BlockSpec((1, N_KV_HEADS, 2, tm), lambda b, i: (b, 0, 0, i)),
            pl.BlockSpec((1, N_KV_HEADS, tm // tk, HEAD_DIM, tk), lambda b, i: (b, 0, i, 0, 0)),
            pl.BlockSpec((1, tm, F_W), lambda b, i: (b, i, 0)),
            pl.BlockSpec((1, tm, F_W), lambda b, i: (b, i, 0)),
        ],
        out_shape=[
            jax.ShapeDtypeStruct((B, N_KV_HEADS, Q_GROUP, HEAD_DIM, S), BF16),
            jax.ShapeDtypeStruct((B, N_KV_HEADS, S, HEAD_DIM), BF16),
            jax.ShapeDtypeStruct((B, N_KV_HEADS, Q_GROUP, QK8_DEPTH, S), F8),
            jax.ShapeDtypeStruct((B, N_KV_HEADS, S, QK8_DEPTH), F8),
            jax.ShapeDtypeStruct((B, N_KV_HEADS, 2, S), F32),
            jax.ShapeDtypeStruct((B, N_KV_HEADS, S // tk, HEAD_DIM, tk), BF16),
            jax.ShapeDtypeStruct((B, S, F_W), F32),
            jax.ShapeDtypeStruct((B, S, F_W), F32),
        ],
        compiler_params=_params("parallel", "parallel"),
        name="proj",
    )(x, g, wqkv_t, wf, qg, kg, cos_t, sin_t, cs)


def _attn_kernel(q_ref, k_ref, q8_ref, k8_ref, v_ref, kn_ref, o_ref, s_buf, cmax_ref, m_ref, acc_ref,
                 *, tk, n_chunks):
    tq = q_ref.shape[-1]
    q_cat = jnp.concatenate([q_ref[0, 0, g] for g in range(Q_GROUP)], axis=1)
    ones = jnp.ones((ONES_ROWS, tk), BF16)

    def scores(j, slot):
        start = pl.multiple_of(j * tk, tk)
        kc = k_ref[0, 0, pl.ds(start, tk), :]
        s_t = jnp.dot(kc, q_cat, preferred_element_type=F32)
        s_buf[slot] = s_t
        cmax_ref[slot] = jnp.max(s_t, axis=0, keepdims=True)

    def accumulate(j, slot):
        m_old = m_ref[...]
        m_new = jnp.maximum(m_old, cmax_ref[slot])
        alpha = jnp.exp2(m_old - m_new)
        p_t = jnp.exp2(s_buf[slot] - m_new).astype(BF16)
        v_ext = jnp.concatenate([v_ref[0, 0, j], ones], axis=0)
        acc_ref[...] = acc_ref[...] * alpha + jnp.dot(v_ext, p_t, preferred_element_type=F32)
        m_ref[...] = m_new

    def general_path():
        m_ref[...] = jnp.full(m_ref.shape, -jnp.inf, F32)
        scores(0, 0)

        def two_chunks(jj, carry):
            j0 = 2 * jj
            scores(j0 + 1, 1)
            accumulate(j0, 0)
            scores(jnp.minimum(j0 + 2, n_chunks - 1), 0)
            accumulate(j0 + 1, 1)
            return carry

        lax.fori_loop(0, n_chunks // 2, two_chunks, 0)

    def bounded_path():
        per_trip = math.gcd(BOUNDED_CHUNKS_PER_TRIP, n_chunks)
        q8_cat = jnp.concatenate([q8_ref[0, 0, g] for g in range(Q_GROUP)], axis=1)

        n_lane_tiles = q8_cat.shape[1] // ATTN_LANE_TILE
        n_tiles = per_trip * n_lane_tiles

        def trip(jj, carry):
            probs = {}

            def emit_scores(t):
                c, n = divmod(t, n_lane_tiles)
                start = pl.multiple_of((jj * per_trip + c) * tk, tk)
                kc = k8_ref[0, 0, pl.ds(start, tk), :]
                q_tile = q8_cat[:, n * ATTN_LANE_TILE:(n + 1) * ATTN_LANE_TILE]
                s_t = jnp.dot(kc, q_tile, preferred_element_type=F32) * (1.0 / Q8_PRESCALE)
                probs[t] = jnp.exp2(s_t).astype(BF16)

            def emit_pv(t):
                c, n = divmod(t, n_lane_tiles)
                v_ext = jnp.concatenate([v_ref[0, 0, jj * per_trip + c], ones], axis=0)
                lanes = slice(n * ATTN_LANE_TILE, (n + 1) * ATTN_LANE_TILE)
                acc_ref[:, lanes] += jnp.dot(v_ext, probs.pop(t), preferred_element_type=F32)

            for t in range(min(BOUNDED_SCORE_LOOKAHEAD, n_tiles)):
                emit_scores(t)
            for t in range(n_tiles):
                emit_pv(t)
                if t + BOUNDED_SCORE_LOOKAHEAD < n_tiles:
                    emit_scores(t + BOUNDED_SCORE_LOOKAHEAD)
            return carry

        lax.fori_loop(0, n_chunks // per_trip, trip, 0)

    acc_ref[...] = jnp.zeros(acc_ref.shape, F32)
    q32 = q_cat.astype(F32)
    q_norm2 = jnp.max(jnp.sum(q32 * q32, axis=0, keepdims=True))
    k_norm2 = jnp.max(kn_ref[0, 0, 0:1])
    v_max = jnp.max(kn_ref[0, 0, 1:2])
    bounded = jnp.logical_and(q_norm2 * k_norm2 <= SAFE_SCORE_BOUND ** 2, v_max <= SAFE_VALUE_BOUND)
    fits_f8 = jnp.logical_and(q_norm2 <= (F8_SAFE_ABS / Q8_PRESCALE) ** 2, k_norm2 <= F8_SAFE_ABS ** 2)
    lax.cond(jnp.logical_and(bounded, fits_f8), bounded_path, general_path)
    acc = acc_ref[...]
    out = acc[:HEAD_DIM] / acc[HEAD_DIM:HEAD_DIM + 1]
    for g in range(Q_GROUP):
        o_ref[0, 0, g] = out[:, g * tq:(g + 1) * tq].astype(o_ref.dtype)


def _attn_call(q_t, k, q8_t, k8, v_t, k_norm2, *, tq):
    B, _, _, _, S = q_t.shape
    n_chunks, tk = v_t.shape[2], v_t.shape[4]
    return pl.pallas_call(
        functools.partial(_attn_kernel, tk=tk, n_chunks=n_chunks),
        grid=(B, N_KV_HEADS, S // tq),
        in_specs=[
            pl.BlockSpec((1, 1, Q_GROUP, HEAD_DIM, tq), lambda b, h, i: (b, h, 0, 0, i)),
            pl.BlockSpec((1, 1, S, HEAD_DIM), lambda b, h, i: (b, h, 0, 0)),
            pl.BlockSpec((1, 1, Q_GROUP, QK8_DEPTH, tq), lambda b, h, i: (b, h, 0, 0, i)),
            pl.BlockSpec((1, 1, S, QK8_DEPTH), lambda b, h, i: (b, h, 0, 0)),
            pl.BlockSpec((1, 1, n_chunks, HEAD_DIM, tk), lambda b, h, i: (b, h, 0, 0, 0)),
            pl.BlockSpec((1, 1, 2, S), lambda b, h, i: (b, h, 0, 0)),
        ],
        out_specs=pl.BlockSpec((1, 1, Q_GROUP, HEAD_DIM, tq), lambda b, h, i: (b, h, 0, 0, i)),
        out_shape=jax.ShapeDtypeStruct((B, N_KV_HEADS, Q_GROUP, HEAD_DIM, S), BF16),
        scratch_shapes=[
            pltpu.VMEM((2, tk, Q_GROUP * tq), F32),
            pltpu.VMEM((2, 1, Q_GROUP * tq), F32),
            pltpu.VMEM((1, Q_GROUP * tq), F32),
            pltpu.VMEM((HEAD_DIM + ONES_ROWS, Q_GROUP * tq), F32),
        ],
        compiler_params=_params("parallel", "parallel", "parallel"),
        name="attn",
    )(q_t, k, q8_t, k8, v_t, k_norm2)


def _fft1_kernel(zr_ref, zi_ref, c1_ref, s1_ref, tc_ref, ts_ref, yr_out, yi_out):
    n1, span, width = zr_ref.shape[1:]
    sub = FFT1_SUBLANE_BLOCK
    rows = n1 * sub
    c1, s1 = c1_ref[...], s1_ref[...]
    dot = functools.partial(jnp.dot, preferred_element_type=F32)
    for t in range(span // sub):
        part = slice(t * sub, (t + 1) * sub)
        zr = zr_ref[0, :, part, :].reshape(rows, width).astype(BF16)
        zi = zi_ref[0, :, part, :].reshape(rows, width).astype(BF16)
        yr = dot(c1, zr) + dot(s1, zi)
        yi = dot(c1, zi) - dot(s1, zr)
        tc = tc_ref[:, part, :].reshape(rows, width)
        ts = ts_ref[:, part, :].reshape(rows, width)
        yr_out[0, :, part, :] = (yr * tc + yi * ts).reshape(n1, sub, width)
        yi_out[0, :, part, :] = (yi * tc - yr * ts).reshape(n1, sub, width)


def _fft1_call(zr, zi, c1, s1, tc, ts):
    B, n1, n2, C = zr.shape
    sub = math.gcd(FFT1_POSITIONS_PER_STEP, n2)
    blk = pl.BlockSpec((1, n1, sub, C), lambda i, b: (b, 0, i, 0))
    tw = pl.BlockSpec((n1, sub, C), lambda i, b: (0, i, 0))
    full = pl.BlockSpec(c1.shape, lambda i, b: (0, 0))
    return pl.pallas_call(
        _fft1_kernel,
        grid=(n2 // sub, B),
        in_specs=[blk, blk, full, full, tw, tw],
        out_specs=[blk, blk],
        out_shape=[jax.ShapeDtypeStruct((B, n1, n2, C), F32)] * 2,
        compiler_params=_params("parallel", "parallel"),
        name="fft_stage1",
    )(zr, zi, c1, s1, tc, ts)


def _fft2_kernel(yr_ref, yi_ref, c2_ref, s2_ref, o_ref, *, group):
    c2, s2 = c2_ref[...], s2_ref[...]
    width = yr_ref.shape[-1]
    for g in range(group):
        yr = yr_ref[0, g].astype(BF16)
        yi = yi_ref[0, g].astype(BF16)
        re = (jnp.dot(c2, yr, preferred_element_type=F32)
              + jnp.dot(s2, yi, preferred_element_type=F32))
        o_ref[0, :, g, :] = re


def _fft2_call(yr, yi, c2, s2, *, group):
    B, n1, n2, C = yr.shape
    blk = pl.BlockSpec((1, group, n2, C), lambda b, i: (b, i, 0, 0))
    full = pl.BlockSpec((n2, n2), lambda b, i: (0, 0))
    return pl.pallas_call(
        functools.partial(_fft2_kernel, group=group),
        grid=(B, n1 // group),
        in_specs=[blk, blk, full, full],
        out_specs=pl.BlockSpec((1, n2, group, C), lambda b, i: (b, 0, i, 0)),
        out_shape=jax.ShapeDtypeStruct((B, n2, n1, C), F32),
        compiler_params=_params("parallel", "parallel"),
        name="fft_stage2",
    )(yr, yi, c2, s2)


def _merge_kernel(x_ref, g_ref, at_ref, four_ref, wgate_ref, bg_ref, wab_ref, wfb_ref, wout_ref, o_ref):
    D = x_ref.shape[-1]
    x = x_ref[0]
    ms = jnp.mean(x * x, axis=-1, keepdims=True)
    h = (x * lax.rsqrt(ms + EPS) * g_ref[...]).astype(BF16)
    gates = jax.nn.sigmoid(jnp.dot(h, wgate_ref[...], preferred_element_type=F32) + bg_ref[...])
    a = lax.dot_general(at_ref[0], wab_ref[...], (((0,), (0,)), ((), ())),
                        preferred_element_type=F32)
    f = jnp.dot(four_ref[0].astype(BF16), wfb_ref[...], preferred_element_type=F32)
    merged = (gates[:, :D] * a + gates[:, D:] * f).astype(BF16)
    o_ref[0] = x + jnp.dot(merged, wout_ref[...], preferred_element_type=F32)


def _merge_call(x, g, attn_t, four, wgate, bg, wab, wfb, wout, *, tm):
    B, S, D = x.shape
    full = lambda shape: pl.BlockSpec(shape, lambda b, i: (0,) * len(shape))
    row = lambda w: pl.BlockSpec((1, tm, w), lambda b, i: (b, i, 0))
    return pl.pallas_call(
        _merge_kernel,
        grid=(B, S // tm),
        in_specs=[
            row(D),
            full((1, D)),
            pl.BlockSpec((1, Q_W, tm), lambda b, i: (b, 0, i)),
            row(F_W),
            full((D, 2 * D)), full((1, 2 * D)),
            full((Q_W, D)), full((F_W, D)), full((D, D)),
        ],
        out_specs=row(D),
        out_shape=jax.ShapeDtypeStruct((B, S, D), F32),
        compiler_params=_params("parallel", "parallel"),
        name="merge",
    )(x, g, attn_t, four, wgate, bg, wab, wfb, wout)


def _mlp_kernel(x_ref, g_ref, wup_ref, wdown_ref, gf_ref, o_ref, *, ff_chunk, final_norm):
    x = x_ref[0]
    ms = jnp.mean(x * x, axis=-1, keepdims=True)
    h = (x * lax.rsqrt(ms + EPS) * g_ref[...]).astype(BF16)
    d_ff = wup_ref.shape[1]
    y = x
    for c in range(d_ff // ff_chunk):
        u = jnp.maximum(jnp.dot(h, wup_ref[:, c * ff_chunk:(c + 1) * ff_chunk],
                                preferred_element_type=F32), 0.0)
        y = y + jnp.dot((u * u).astype(BF16), wdown_ref[c * ff_chunk:(c + 1) * ff_chunk, :],
                        preferred_element_type=F32)
    if final_norm:
        ms = jnp.mean(y * y, axis=-1, keepdims=True)
        y = y * lax.rsqrt(ms + EPS) * gf_ref[...]
    o_ref[0] = y


def _mlp_call(x, g, wup, wdown, gf, *, tm, final_norm):
    B, S, D = x.shape
    d_ff = wup.shape[1]
    full = lambda shape: pl.BlockSpec(shape, lambda b, i: (0,) * len(shape))
    row = pl.BlockSpec((1, tm, D), lambda b, i: (b, i, 0))
    return pl.pallas_call(
        functools.partial(_mlp_kernel, ff_chunk=min(1024, d_ff), final_norm=final_norm),
        grid=(B, S // tm),
        in_specs=[row, full((1, D)), full((D, d_ff)), full((d_ff, D)), full((1, D))],
        out_specs=row,
        out_shape=jax.ShapeDtypeStruct((B, S, D), F32),
        compiler_params=_params("parallel", "parallel"),
        name="mlp",
    )(x, g, wup, wdown, gf)


def _rope_tables_t(seq):
    n_rows = seq // GRID_W
    rows = jnp.repeat(jnp.arange(n_rows, dtype=F32), GRID_W)
    cols = jnp.tile(jnp.arange(GRID_W, dtype=F32), n_rows)
    inv_freq = ROPE_THETA ** (-jnp.arange(ROPE_PAIRS_PER_AXIS, dtype=F32) / ROPE_PAIRS_PER_AXIS)
    ang = jnp.concatenate([rows[:, None] * inv_freq[None, :], cols[:, None] * inv_freq[None, :]], axis=-1)
    return jnp.cos(ang).T, jnp.sin(ang).T


def _dft_cos_sin(n, scale):
    idx = np.arange(n)
    ang = 2.0 * np.pi * ((idx[:, None] * idx[None, :]) % n) / n
    return np.cos(ang) * scale, np.sin(ang) * scale


def _fourier_constants(seq, n1, n2):
    gc, gs = _dft_cos_sin(FOURIER_GROUP_W, FOURIER_GROUP_W ** -0.5)
    eye = np.eye(N_FOURIER_GROUPS)
    cs = np.concatenate([np.kron(eye, gc), -np.kron(eye, gs)], axis=1)
    c1, s1 = (np.kron(m, np.eye(FFT1_SUBLANE_BLOCK)) for m in _dft_cos_sin(n1, n1 ** -0.5))
    c2, s2 = _dft_cos_sin(n2, n2 ** -0.5)
    k1 = np.arange(n1)[:, None]
    s2_idx = np.arange(n2)[None, :]
    ang = 2.0 * np.pi * ((k1 * s2_idx) % seq) / seq
    tc = np.repeat(np.cos(ang)[:, :, None], F_W, axis=2)
    ts = np.repeat(np.sin(ang)[:, :, None], F_W, axis=2)
    f32 = lambda a: jnp.asarray(a, dtype=F32)
    bf = lambda a: f32(a).astype(BF16)
    return bf(cs), bf(c1), bf(s1), bf(c2), bf(s2), f32(tc), f32(ts)


def _pair_split_perm():
    return np.concatenate([np.arange(0, HEAD_DIM, 2), np.arange(1, HEAD_DIM, 2)])


def kernel(x, norm_mix, w_in, b_gate, q_gain, k_gain, w_attn_branch, w_fourier_branch, w_out,
           norm_mlp, w_up, w_down, norm_final):
    B, S, D = x.shape
    depth = w_in.shape[0]
    tm, tp, tk, tq, n1, n2 = _tiles(S)

    perm = _pair_split_perm()
    q_cols = (np.arange(N_HEADS)[:, None] * HEAD_DIM + perm[None, :]).reshape(-1)
    k_cols = Q_W + (np.arange(N_KV_HEADS)[:, None] * HEAD_DIM + perm[None, :]).reshape(-1)
    v_cols = Q_W + KV_W + np.arange(KV_W)
    qkv_cols = np.concatenate([q_cols, k_cols, v_cols])
    fg_start = Q_W + 2 * KV_W

    cos_t, sin_t = _rope_tables_t(S)
    cs, c1, s1, c2, s2, tc, ts = _fourier_constants(S, n1, n2)

    for l in range(depth):
        wqkv_t = w_in[l][:, qkv_cols].T.astype(BF16)
        wf = w_in[l][:, fg_start:fg_start + F_W].astype(BF16)
        wgate = w_in[l][:, fg_start + F_W:].astype(BF16)
        qg = q_gain[l][perm][:, None]
        kg = k_gain[l][perm][:, None]
        q_t, k, q8_t, k8, k_norm2, v_t, zr, zi = _proj_call(
            x, norm_mix[l][None], wqkv_t, wf, qg, kg, cos_t, sin_t, cs, tm=tp, tk=tk)

        attn_t = _attn_call(q_t, k, q8_t, k8, v_t, k_norm2, tq=tq)
        attn_t = attn_t.reshape(B, Q_W, S)

        yr, yi = _fft1_call(zr.reshape(B, n1, n2, F_W), zi.reshape(B, n1, n2, F_W), c1, s1, tc, ts)
        four = _fft2_call(yr, yi, c2, s2, group=min(8, n1))
        four = four.reshape(B, S, F_W)

        x = _merge_call(x, norm_mix[l][None], attn_t, four, wgate, b_gate[l][None],
                        w_attn_branch[l].astype(BF16), w_fourier_branch[l].astype(BF16),
                        w_out[l].astype(BF16), tm=tm)
        x = _mlp_call(x, norm_mlp[l][None], w_up[l].astype(BF16), w_down[l].astype(BF16),
                      norm_final[None], tm=tm, final_norm=(l == depth - 1))
    return x
```

```python
import functools
import math

import numpy as np
import jax
import jax.numpy as jnp
from jax import lax
from jax.experimental import pallas as pl
from jax.experimental.pallas import tpu as pltpu

HEAD_DIM = 64
N_HEADS = 12
N_KV_HEADS = 4
Q_GROUP = N_HEADS // N_KV_HEADS
FOURIER_GROUP_W = 64
N_FOURIER_GROUPS = 4
Q_W = N_HEADS * HEAD_DIM
KV_W = N_KV_HEADS * HEAD_DIM
F_W = N_FOURIER_GROUPS * FOURIER_GROUP_W
GRID_W = 64
ROPE_THETA = 10000.0
ROPE_PAIRS_PER_AXIS = HEAD_DIM // 4
EPS = 1e-6
HALF = HEAD_DIM // 2

V7X_VMEM_LIMIT_BYTES = 56 * 1024 * 1024
ONES_ROWS = 16
Q_SCALE = HEAD_DIM ** -0.5 * math.log2(math.e)
SAFE_SCORE_BOUND = 48.0
SAFE_VALUE_BOUND = 2.0 ** 40
BOUNDED_CHUNKS_PER_TRIP = 32
ATTN_LANE_TILE = 256
BOUNDED_SCORE_LOOKAHEAD = 4
FFT1_SUBLANE_BLOCK = 8
FFT1_POSITIONS_PER_STEP = 32

BF16 = jnp.bfloat16
F32 = jnp.float32
F8 = jnp.float8_e4m3fn
QK8_DEPTH = 4 * HEAD_DIM
Q8_PRESCALE = 16.0
F8_SAFE_ABS = 400.0


def _tiles(seq):
    tm = min(512, seq)
    tp = min(1024, seq)
    tk = min(256, seq)
    tq = min(1024, seq)
    n1 = 64
    n2 = seq // n1
    assert seq % tm == 0 and seq % tp == 0 and tp % tk == 0 and seq % tq == 0 and n1 * n2 == seq
    assert (seq // tk) % 2 == 0
    return tm, tp, tk, tq, n1, n2


def _params(*sem):
    return pltpu.CompilerParams(dimension_semantics=sem, vmem_limit_bytes=V7X_VMEM_LIMIT_BYTES)


def _proj_kernel(x_ref, g_ref, wqkv_ref, wf_ref, qg_ref, kg_ref, cos_ref, sin_ref, cs_ref,
                 q_out, k_out, q8_out, k8_out, kn_out, v_out, zr_out, zi_out, *, tk):
    x = x_ref[0]
    tm = x.shape[0]
    ms = jnp.mean(x * x, axis=-1, keepdims=True)
    h = (x * lax.rsqrt(ms + EPS) * g_ref[...]).astype(BF16)

    qkv_t = lax.dot_general(wqkv_ref[...], h, (((1,), (1,)), ((), ())),
                            preferred_element_type=F32)
    cos = cos_ref[...][None]
    sin = sin_ref[...][None]

    def norm_rope(t, gain, n_heads, scale):
        t = t.reshape(n_heads, HEAD_DIM, tm)
        msq = jnp.mean(t * t, axis=1, keepdims=True)
        y = t * lax.rsqrt(msq + EPS) * gain[None]
        a, b = y[:, :HALF], y[:, HALF:]
        out = jnp.concatenate([a * cos - b * sin, a * sin + b * cos], axis=1)
        return out * scale if scale != 1.0 else out

    q = norm_rope(qkv_t[:Q_W], qg_ref[...], N_HEADS, Q_SCALE)
    q_out[0] = q.reshape(N_KV_HEADS, Q_GROUP, HEAD_DIM, tm).astype(BF16)

    def split8(t):
        hi = t.astype(F8).astype(F32)
        return hi, (t - hi).astype(F8).astype(F32)

    q_hi, q_lo = split8(q * Q8_PRESCALE)
    zq = jnp.zeros_like(q_hi)
    q8 = jnp.concatenate([q_hi, q_hi, q_lo, zq], axis=1).astype(F8)
    q8_out[0] = q8.reshape(N_KV_HEADS, Q_GROUP, QK8_DEPTH, tm)

    k = norm_rope(qkv_t[Q_W:Q_W + KV_W], kg_ref[...], N_KV_HEADS, 1.0)
    k_used = k.astype(BF16).astype(F32)
    kn_out[0, :, 0:1] = jnp.sum(k_used * k_used, axis=1, keepdims=True)
    k_rows = k.reshape(KV_W, tm).T.astype(BF16)
    for hk in range(N_KV_HEADS):
        k_out[0, hk] = k_rows[:, hk * HEAD_DIM:(hk + 1) * HEAD_DIM]
    k_hi, k_lo = split8(k)
    k8_rows = jnp.concatenate([k_hi, k_lo, k_hi, jnp.zeros_like(k_hi)], axis=1)
    k8_rows = k8_rows.reshape(N_KV_HEADS * QK8_DEPTH, tm).T.astype(F8)
    for hk in range(N_KV_HEADS):
        k8_out[0, hk] = k8_rows[:, hk * QK8_DEPTH:(hk + 1) * QK8_DEPTH]

    v_t = qkv_t[Q_W + KV_W:].astype(BF16).reshape(N_KV_HEADS, HEAD_DIM, tm)
    kn_out[0, :, 1:2] = jnp.max(jnp.abs(v_t.astype(F32)), axis=1, keepdims=True)
    for c in range(tm // tk):
        v_out[0, :, c] = v_t[:, :, c * tk:(c + 1) * tk]

    f = jnp.dot(h, wf_ref[...], preferred_element_type=F32)
    z = jnp.dot(f.astype(BF16), cs_ref[...], preferred_element_type=F32)
    zr_out[0] = z[:, :F_W]
    zi_out[0] = z[:, F_W:]


def _proj_call(x, g, wqkv_t, wf, qg, kg, cos_t, sin_t, cs, *, tm, tk):
    B, S, D = x.shape
    n_rows = wqkv_t.shape[0]
    full = lambda shape: pl.BlockSpec(shape, lambda b, i: (0,) * len(shape))
    return pl.pallas_call(
        functools.partial(_proj_kernel, tk=tk),
        grid=(B, S // tm),
        in_specs=[
            pl.BlockSpec((1, tm, D), lambda b, i: (b, i, 0)),
            full((1, D)),
            full((n_rows, D)),
            full((D, F_W)),
            full((HEAD_DIM, 1)),
            full((HEAD_DIM, 1)),
            pl.BlockSpec((HALF, tm), lambda b, i: (0, i)),
            pl.BlockSpec((HALF, tm), lambda b, i: (0, i)),
            full((F_W, 2 * F_W)),
        ],
        out_specs=[
            pl.BlockSpec((1, N_KV_HEADS, Q_GROUP, HEAD_DIM, tm), lambda b, i: (b, 0, 0, 0, i)),
            pl.BlockSpec((1, N_KV_HEADS, tm, HEAD_DIM), lambda b, i: (b, 0, i, 0)),
            pl.BlockSpec((1, N_KV_HEADS, Q_GROUP, QK8_DEPTH, tm), lambda b, i: (b, 0, 0, 0, i)),
            pl.BlockSpec((1, N_KV_HEADS, tm, QK8_DEPTH), lambda b, i: (b, 0, i, 0)),
            pl.BlockSpec((1, N_KV_HEADS, 2, tm), lambda b, i: (b, 0, 0, i)),
            pl.BlockSpec((1, N_KV_HEADS, tm // tk, HEAD_DIM, tk), lambda b, i: (b, 0, i, 0, 0)),
            pl.BlockSpec((1, tm, F_W), lambda b, i: (b, i, 0)),
            pl.BlockSpec((1, tm, F_W), lambda b, i: (b, i, 0)),
        ],
        out_shape=[
            jax.ShapeDtypeStruct((B, N_KV_HEADS, Q_GROUP, HEAD_DIM, S), BF16),
            jax.ShapeDtypeStruct((B, N_KV_HEADS, S, HEAD_DIM), BF16),
            jax.ShapeDtypeStruct((B, N_KV_HEADS, Q_GROUP, QK8_DEPTH, S), F8),
            jax.ShapeDtypeStruct((B, N_KV_HEADS, S, QK8_DEPTH), F8),
            jax.ShapeDtypeStruct((B, N_KV_HEADS, 2, S), F32),
            jax.ShapeDtypeStruct((B, N_KV_HEADS, S // tk, HEAD_DIM, tk), BF16),
            jax.ShapeDtypeStruct((B, S, F_W), F32),
            jax.ShapeDtypeStruct((B, S, F_W), F32),
        ],
        compiler_params=_params("parallel", "parallel"),
        name="proj",
    )(x, g, wqkv_t, wf, qg, kg, cos_t, sin_t, cs)


def _attn_kernel(q_ref, k_ref, q8_ref, k8_ref, v_ref, kn_ref, o_ref, s_buf, cmax_ref, m_ref, acc_ref,
                 *, tk, n_chunks):
    tq = q_ref.shape[-1]
    q_cat = jnp.concatenate([q_ref[0, 0, g] for g in range(Q_GROUP)], axis=1)
    ones = jnp.ones((ONES_ROWS, tk), BF16)

    def scores(j, slot):
        start = pl.multiple_of(j * tk, tk)
        kc = k_ref[0, 0, pl.ds(start, tk), :]
        s_t = jnp.dot(kc, q_cat, preferred_element_type=F32)
        s_buf[slot] = s_t
        cmax_ref[slot] = jnp.max(s_t, axis=0, keepdims=True)

    def accumulate(j, slot):
        m_old = m_ref[...]
        m_new = jnp.maximum(m_old, cmax_ref[slot])
        alpha = jnp.exp2(m_old - m_new)
        p_t = jnp.exp2(s_buf[slot] - m_new).astype(BF16)
        v_ext = jnp.concatenate([v_ref[0, 0, j], ones], axis=0)
        acc_ref[...] = acc_ref[...] * alpha + jnp.dot(v_ext, p_t, preferred_element_type=F32)
        m_ref[...] = m_new

    def general_path():
        m_ref[...] = jnp.full(m_ref.shape, -jnp.inf, F32)
        scores(0, 0)

        def two_chunks(jj, carry):
            j0 = 2 * jj
            scores(j0 + 1, 1)
            accumulate(j0, 0)
            scores(jnp.minimum(j0 + 2, n_chunks - 1), 0)
            accumulate(j0 + 1, 1)
            return carry

        lax.fori_loop(0, n_chunks // 2, two_chunks, 0)

    def bounded_path():
        per_trip = math.gcd(BOUNDED_CHUNKS_PER_TRIP, n_chunks)
        q8_cat = jnp.concatenate([q8_ref[0, 0, g] for g in range(Q_GROUP)], axis=1)

        n_lane_tiles = q8_cat.shape[1] // ATTN_LANE_TILE
        n_tiles = per_trip * n_lane_tiles

        def trip(jj, carry):
            probs = {}

            def emit_scores(t):
                c, n = divmod(t, n_lane_tiles)
                start = pl.multiple_of((jj * per_trip + c) * tk, tk)
                kc = k8_ref[0, 0, pl.ds(start, tk), :]
                q_tile = q8_cat[:, n * ATTN_LANE_TILE:(n + 1) * ATTN_LANE_TILE]
                s_t = jnp.dot(kc, q_tile, preferred_element_type=F32) * (1.0 / Q8_PRESCALE)
                probs[t] = jnp.exp2(s_t).astype(BF16)

            def emit_pv(t):
                c, n = divmod(t, n_lane_tiles)
                v_ext = jnp.concatenate([v_ref[0, 0, jj * per_trip + c], ones], axis=0)
                lanes = slice(n * ATTN_LANE_TILE, (n + 1) * ATTN_LANE_TILE)
                acc_ref[:, lanes] += jnp.dot(v_ext, probs.pop(t), preferred_element_type=F32)

            for t in range(min(BOUNDED_SCORE_LOOKAHEAD, n_tiles)):
                emit_scores(t)
            for t in range(n_tiles):
                emit_pv(t)
                if t + BOUNDED_SCORE_LOOKAHEAD < n_tiles:
                    emit_scores(t + BOUNDED_SCORE_LOOKAHEAD)
            return carry

        lax.fori_loop(0, n_chunks // per_trip, trip, 0)

    acc_ref[...] = jnp.zeros(acc_ref.shape, F32)
    q32 = q_cat.astype(F32)
    q_norm2 = jnp.max(jnp.sum(q32 * q32, axis=0, keepdims=True))
    k_norm2 = jnp.max(kn_ref[0, 0, 0:1])
    v_max = jnp.max(kn_ref[0, 0, 1:2])
    bounded = jnp.logical_and(q_norm2 * k_norm2 <= SAFE_SCORE_BOUND ** 2, v_max <= SAFE_VALUE_BOUND)
    fits_f8 = jnp.logical_and(q_norm2 <= (F8_SAFE_ABS / Q8_PRESCALE) ** 2, k_norm2 <= F8_SAFE_ABS ** 2)
    lax.cond(jnp.logical_and(bounded, fits_f8), bounded_path, general_path)
    acc = acc_ref[...]
    out = acc[:HEAD_DIM] / acc[HEAD_DIM:HEAD_DIM + 1]
    for g in range(Q_GROUP):
        o_ref[0, 0, g] = out[:, g * tq:(g + 1) * tq].astype(o_ref.dtype)


def _attn_call(q_t, k, q8_t, k8, v_t, k_norm2, *, tq):
    B, _, _, _, S = q_t.shape
    n_chunks, tk = v_t.shape[2], v_t.shape[4]
    return pl.pallas_call(
        functools.partial(_attn_kernel, tk=tk, n_chunks=n_chunks),
        grid=(B, N_KV_HEADS, S // tq),
        in_specs=[
            pl.BlockSpec((1, 1, Q_GROUP, HEAD_DIM, tq), lambda b, h, i: (b, h, 0, 0, i)),
            pl.BlockSpec((1, 1, S, HEAD_DIM), lambda b, h, i: (b, h, 0, 0)),
            pl.BlockSpec((1, 1, Q_GROUP, QK8_DEPTH, tq), lambda b, h, i: (b, h, 0, 0, i)),
            pl.BlockSpec((1, 1, S, QK8_DEPTH), lambda b, h, i: (b, h, 0, 0)),
            pl.BlockSpec((1, 1, n_chunks, HEAD_DIM, tk), lambda b, h, i: (b, h, 0, 0, 0)),
            pl.BlockSpec((1, 1, 2, S), lambda b, h, i: (b, h, 0, 0)),
        ],
        out_specs=pl.BlockSpec((1, 1, Q_GROUP, HEAD_DIM, tq), lambda b, h, i: (b, h, 0, 0, i)),
        out_shape=jax.ShapeDtypeStruct((B, N_KV_HEADS, Q_GROUP, HEAD_DIM, S), BF16),
        scratch_shapes=[
            pltpu.VMEM((2, tk, Q_GROUP * tq), F32),
            pltpu.VMEM((2, 1, Q_GROUP * tq), F32),
            pltpu.VMEM((1, Q_GROUP * tq), F32),
            pltpu.VMEM((HEAD_DIM + ONES_ROWS, Q_GROUP * tq), F32),
        ],
        compiler_params=_params("parallel", "parallel", "parallel"),
        name="attn",
    )(q_t, k, q8_t, k8, v_t, k_norm2)


def _fft1_kernel(zr_ref, zi_ref, c1_ref, s1_ref, tc_ref, ts_ref, yr_out, yi_out):
    n1, span, width = zr_ref.shape[1:]
    sub = FFT1_SUBLANE_BLOCK
    rows = n1 * sub
    c1, s1 = c1_ref[...], s1_ref[...]
    dot = functools.partial(jnp.dot, preferred_element_type=F32)
    for t in range(span // sub):
        part = slice(t * sub, (t + 1) * sub)
        zr = zr_ref[0, :, part, :].reshape(rows, width).astype(BF16)
        zi = zi_ref[0, :, part, :].reshape(rows, width).astype(BF16)
        yr = dot(c1, zr) + dot(s1, zi)
        yi = dot(c1, zi) - dot(s1, zr)
        tc = tc_ref[:, part, :].reshape(rows, width)
        ts = ts_ref[:, part, :].reshape(rows, width)
        yr_out[0, :, part, :] = (yr * tc + yi * ts).reshape(n1, sub, width)
        yi_out[0, :, part, :] = (yi * tc - yr * ts).reshape(n1, sub, width)


def _fft1_call(zr, zi, c1, s1, tc, ts):
    B, n1, n2, C = zr.shape
    sub = math.gcd(FFT1_POSITIONS_PER_STEP, n2)
    blk = pl.BlockSpec((1, n1, sub, C), lambda i, b: (b, 0, i, 0))
    tw = pl.BlockSpec((n1, sub, C), lambda i, b: (0, i, 0))
    full = pl.BlockSpec(c1.shape, lambda i, b: (0, 0))
    return pl.pallas_call(
        _fft1_kernel,
        grid=(n2 // sub, B),
        in_specs=[blk, blk, full, full, tw, tw],
        out_specs=[blk, blk],
        out_shape=[jax.ShapeDtypeStruct((B, n1, n2, C), F32)] * 2,
        compiler_params=_params("parallel", "parallel"),
        name="fft_stage1",
    )(zr, zi, c1, s1, tc, ts)


def _fft2_kernel(yr_ref, yi_ref, c2_ref, s2_ref, o_ref, *, group):
    c2, s2 = c2_ref[...], s2_ref[...]
    width = yr_ref.shape[-1]
    for g in range(group):
        yr = yr_ref[0, g].astype(BF16)
        yi = yi_ref[0, g].astype(BF16)
        re = (jnp.dot(c2, yr, preferred_element_type=F32)
              + jnp.dot(s2, yi, preferred_element_type=F32))
        o_ref[0, :, g, :] = re


def _fft2_call(yr, yi, c2, s2, *, group):
    B, n1, n2, C = yr.shape
    blk = pl.BlockSpec((1, group, n2, C), lambda b, i: (b, i, 0, 0))
    full = pl.BlockSpec((n2, n2), lambda b, i: (0, 0))
    return pl.pallas_call(
        functools.partial(_fft2_kernel, group=group),
        grid=(B, n1 // group),
        in_specs=[blk, blk, full, full],
        out_specs=pl.BlockSpec((1, n2, group, C), lambda b, i: (b, 0, i, 0)),
        out_shape=jax.ShapeDtypeStruct((B, n2, n1, C), F32),
        compiler_params=_params("parallel", "parallel"),
        name="fft_stage2",
    )(yr, yi, c2, s2)


def _merge_kernel(x_ref, g_ref, at_ref, four_ref, wgate_ref, bg_ref, wab_ref, wfb_ref, wout_ref, o_ref):
    D = x_ref.shape[-1]
    x = x_ref[0]
    ms = jnp.mean(x * x, axis=-1, keepdims=True)
    h = (x * lax.rsqrt(ms + EPS) * g_ref[...]).astype(BF16)
    gates = jax.nn.sigmoid(jnp.dot(h, wgate_ref[...], preferred_element_type=F32) + bg_ref[...])
    a = lax.dot_general(at_ref[0], wab_ref[...], (((0,), (0,)), ((), ())),
                        preferred_element_type=F32)
    f = jnp.dot(four_ref[0].astype(BF16), wfb_ref[...], preferred_element_type=F32)
    merged = (gates[:, :D] * a + gates[:, D:] * f).astype(BF16)
    o_ref[0] = x + jnp.dot(merged, wout_ref[...], preferred_element_type=F32)


def _merge_call(x, g, attn_t, four, wgate, bg, wab, wfb, wout, *, tm):
    B, S, D = x.shape
    full = lambda shape: pl.BlockSpec(shape, lambda b, i: (0,) * len(shape))
    row = lambda w: pl.BlockSpec((1, tm, w), lambda b, i: (b, i, 0))
    return pl.pallas_call(
        _merge_kernel,
        grid=(B, S // tm),
        in_specs=[
            row(D),
            full((1, D)),
            pl.BlockSpec((1, Q_W, tm), lambda b, i: (b, 0, i)),
            row(F_W),
            full((D, 2 * D)), full((1, 2 * D)),
            full((Q_W, D)), full((F_W, D)), full((D, D)),
        ],
        out_specs=row(D),
        out_shape=jax.ShapeDtypeStruct((B, S, D), F32),
        compiler_params=_params("parallel", "parallel"),
        name="merge",
    )(x, g, attn_t, four, wgate, bg, wab, wfb, wout)


def _mlp_kernel(x_ref, g_ref, wup_ref, wdown_ref, gf_ref, o_ref, *, ff_chunk, final_norm):
    x = x_ref[0]
    ms = jnp.mean(x * x, axis=-1, keepdims=True)
    h = (x * lax.rsqrt(ms + EPS) * g_ref[...]).astype(BF16)
    d_ff = wup_ref.shape[1]
    y = x
    for c in range(d_ff // ff_chunk):
        u = jnp.maximum(jnp.dot(h, wup_ref[:, c * ff_chunk:(c + 1) * ff_chunk],
                                preferred_element_type=F32), 0.0)
        y = y + jnp.dot((u * u).astype(BF16), wdown_ref[c * ff_chunk:(c + 1) * ff_chunk, :],
                        preferred_element_type=F32)
    if final_norm:
        ms = jnp.mean(y * y, axis=-1, keepdims=True)
        y = y * lax.rsqrt(ms + EPS) * gf_ref[...]
    o_ref[0] = y


def _mlp_call(x, g, wup, wdown, gf, *, tm, final_norm):
    B, S, D = x.shape
    d_ff = wup.shape[1]
    full = lambda shape: pl.BlockSpec(shape, lambda b, i: (0,) * len(shape))
    row = pl.BlockSpec((1, tm, D), lambda b, i: (b, i, 0))
    return pl.pallas_call(
        functools.partial(_mlp_kernel, ff_chunk=min(1024, d_ff), final_norm=final_norm),
        grid=(B, S // tm),
        in_specs=[row, full((1, D)), full((D, d_ff)), full((d_ff, D)), full((1, D))],
        out_specs=row,
        out_shape=jax.ShapeDtypeStruct((B, S, D), F32),
        compiler_params=_params("parallel", "parallel"),
        name="mlp",
    )(x, g, wup, wdown, gf)


def _rope_tables_t(seq):
    n_rows = seq // GRID_W
    rows = jnp.repeat(jnp.arange(n_rows, dtype=F32), GRID_W)
    cols = jnp.tile(jnp.arange(GRID_W, dtype=F32), n_rows)
    inv_freq = ROPE_THETA ** (-jnp.arange(ROPE_PAIRS_PER_AXIS, dtype=F32) / ROPE_PAIRS_PER_AXIS)
    ang = jnp.concatenate([rows[:, None] * inv_freq[None, :], cols[:, None] * inv_freq[None, :]], axis=-1)
    return jnp.cos(ang).T, jnp.sin(ang).T


def _dft_cos_sin(n, scale):
    idx = np.arange(n)
    ang = 2.0 * np.pi * ((idx[:, None] * idx[None, :]) % n) / n
    return np.cos(ang) * scale, np.sin(ang) * scale


def _fourier_constants(seq, n1, n2):
    gc, gs = _dft_cos_sin(FOURIER_GROUP_W, FOURIER_GROUP_W ** -0.5)
    eye = np.eye(N_FOURIER_GROUPS)
    cs = np.concatenate([np.kron(eye, gc), -np.kron(eye, gs)], axis=1)
    c1, s1 = (np.kron(m, np.eye(FFT1_SUBLANE_BLOCK)) for m in _dft_cos_sin(n1, n1 ** -0.5))
    c2, s2 = _dft_cos_sin(n2, n2 ** -0.5)
    k1 = np.arange(n1)[:, None]
    s2_idx = np.arange(n2)[None, :]
    ang = 2.0 * np.pi * ((k1 * s2_idx) % seq) / seq
    tc = np.repeat(np.cos(ang)[:, :, None], F_W, axis=2)
    ts = np.repeat(np.sin(ang)[:, :, None], F_W, axis=2)
    f32 = lambda a: jnp.asarray(a, dtype=F32)
    bf = lambda a: f32(a).astype(BF16)
    return bf(cs), bf(c1), bf(s1), bf(c2), bf(s2), f32(tc), f32(ts)


def _pair_split_perm():
    return np.concatenate([np.arange(0, HEAD_DIM, 2), np.arange(1, HEAD_DIM, 2)])


def kernel(x, norm_mix, w_in, b_gate, q_gain, k_gain, w_attn_branch, w_fourier_branch, w_out,
           norm_mlp, w_up, w_down, norm_final):
    B, S, D = x.shape
    depth = w_in.shape[0]
    tm, tp, tk, tq, n1, n2 = _tiles(S)

    perm = _pair_split_perm()
    rot_w = Q_W + KV_W
    fg_start = Q_W + 2 * KV_W

    w_in16 = w_in.astype(BF16)
    w_rot = w_in16[:, :, :rot_w].reshape(depth, D, rot_w // HEAD_DIM, HALF, 2)
    w_rot_t = jnp.transpose(w_rot, (0, 2, 4, 3, 1)).reshape(depth, rot_w, D)
    w_v_t = jnp.swapaxes(w_in16[:, :, rot_w:fg_start], 1, 2)
    wqkv_t_all = jnp.concatenate([w_rot_t, w_v_t], axis=1)

    cos_t, sin_t = _rope_tables_t(S)
    cs, c1, s1, c2, s2, tc, ts = _fourier_constants(S, n1, n2)

    for l in range(depth):
        wqkv_t = wqkv_t_all[l]
        wf = w_in16[l][:, fg_start:fg_start + F_W]
        wgate = w_in16[l][:, fg_start + F_W:]
        qg = q_gain[l][perm][:, None]
        kg = k_gain[l][perm][:, None]
        q_t, k, q8_t, k8, k_norm2, v_t, zr, zi = _proj_call(
            x, norm_mix[l][None], wqkv_t, wf, qg, kg, cos_t, sin_t, cs, tm=tp, tk=tk)

        attn_t = _attn_call(q_t, k, q8_t, k8, v_t, k_norm2, tq=tq)
        attn_t = attn_t.reshape(B, Q_W, S)

        yr, yi = _fft1_call(zr.reshape(B, n1, n2, F_W), zi.reshape(B, n1, n2, F_W), c1, s1, tc, ts)
        four = _fft2_call(yr, yi, c2, s2, group=min(8, n1))
        four = four.reshape(B, S, F_W)

        x = _merge_call(x, norm_mix[l][None], attn_t, four, wgate, b_gate[l][None],
                        w_attn_branch[l].astype(BF16), w_fourier_branch[l].astype(BF16),
                        w_out[l].astype(BF16), tm=tm)
        x = _mlp_call(x, norm_mlp[l][None], w_up[l].astype(BF16), w_down[l].astype(BF16),
                      norm_final[None], tm=tm, final_norm=(l == depth - 1))
    return x
```

```python
import functools
import math

import numpy as np
import jax
import jax.numpy as jnp
from jax import lax
from jax.experimental import pallas as pl
from jax.experimental.pallas import tpu as pltpu

HEAD_DIM = 64
N_HEADS = 12
N_KV_HEADS = 4
Q_GROUP = N_HEADS // N_KV_HEADS
FOURIER_GROUP_W = 64
N_FOURIER_GROUPS = 4
Q_W = N_HEADS * HEAD_DIM
KV_W = N_KV_HEADS * HEAD_DIM
F_W = N_FOURIER_GROUPS * FOURIER_GROUP_W
GRID_W = 64
ROPE_THETA = 10000.0
ROPE_PAIRS_PER_AXIS = HEAD_DIM // 4
EPS = 1e-6
HALF = HEAD_DIM // 2

V7X_VMEM_LIMIT_BYTES = 56 * 1024 * 1024
ONES_ROWS = 16
Q_SCALE = HEAD_DIM ** -0.5 * math.log2(math.e)
SAFE_SCORE_BOUND = 48.0
SAFE_VALUE_BOUND = 2.0 ** 40
BOUNDED_CHUNKS_PER_TRIP = 32
ATTN_LANE_TILE = 256
BOUNDED_SCORE_LOOKAHEAD = 4
FFT1_SUBLANE_BLOCK = 8
FFT1_POSITIONS_PER_STEP = 32

BF16 = jnp.bfloat16
F32 = jnp.float32
F8 = jnp.float8_e4m3fn
QK8_DEPTH = 4 * HEAD_DIM
Q8_PRESCALE = 16.0
F8_SAFE_ABS = 400.0


def _tiles(seq):
    tm = min(512, seq)
    tp = min(1024, seq)
    tk = min(256, seq)
    tq = min(1024, seq)
    n1 = 64
    n2 = seq // n1
    assert seq % tm == 0 and seq % tp == 0 and tp % tk == 0 and seq % tq == 0 and n1 * n2 == seq
    assert (seq // tk) % 2 == 0
    return tm, tp, tk, tq, n1, n2


def _params(*sem):
    return pltpu.CompilerParams(dimension_semantics=sem, vmem_limit_bytes=V7X_VMEM_LIMIT_BYTES)


def _proj_kernel(x_ref, g_ref, wqkv_ref, wf_ref, qg_ref, kg_ref, cos_ref, sin_ref, cs_ref,
                 q_out, k_out, q8_out, k8_out, kn_out, v_out, zr_out, zi_out, *, tk):
    x = x_ref[0]
    tm = x.shape[0]
    ms = jnp.mean(x * x, axis=-1, keepdims=True)
    h = (x * lax.rsqrt(ms + EPS) * g_ref[...]).astype(BF16)

    qkv_t = lax.dot_general(wqkv_ref[...], h, (((1,), (1,)), ((), ())),
                            preferred_element_type=F32)
    cos = cos_ref[...][None]
    sin = sin_ref[...][None]

    def norm_rope(t, gain, n_heads, scale):
        t = t.reshape(n_heads, HEAD_DIM, tm)
        msq = jnp.mean(t * t, axis=1, keepdims=True)
        y = t * lax.rsqrt(msq + EPS) * gain[None]
        a, b = y[:, :HALF], y[:, HALF:]
        out = jnp.concatenate([a * cos - b * sin, a * sin + b * cos], axis=1)
        return out * scale if scale != 1.0 else out

    q = norm_rope(qkv_t[:Q_W], qg_ref[...], N_HEADS, Q_SCALE)
    q_out[0] = q.reshape(N_KV_HEADS, Q_GROUP, HEAD_DIM, tm).astype(BF16)

    def split8(t):
        hi = t.astype(F8).astype(F32)
        return hi, (t - hi).astype(F8).astype(F32)

    q_hi, q_lo = split8(q * Q8_PRESCALE)
    zq = jnp.zeros_like(q_hi)
    q8 = jnp.concatenate([q_hi, q_hi, q_lo, zq], axis=1).astype(F8)
    q8_out[0] = q8.reshape(N_KV_HEADS, Q_GROUP, QK8_DEPTH, tm)

    k = norm_rope(qkv_t[Q_W:Q_W + KV_W], kg_ref[...], N_KV_HEADS, 1.0)
    k_used = k.astype(BF16).astype(F32)
    kn_out[0, :, 0:1] = jnp.sum(k_used * k_used, axis=1, keepdims=True)
    k_rows = k.reshape(KV_W, tm).T.astype(BF16)
    for hk in range(N_KV_HEADS):
        k_out[0, hk] = k_rows[:, hk * HEAD_DIM:(hk + 1) * HEAD_DIM]
    k_hi, k_lo = split8(k)
    k8_rows = jnp.concatenate([k_hi, k_lo, k_hi, jnp.zeros_like(k_hi)], axis=1)
    k8_rows = k8_rows.reshape(N_KV_HEADS * QK8_DEPTH, tm).T.astype(F8)
    for hk in range(N_KV_HEADS):
        k8_out[0, hk] = k8_rows[:, hk * QK8_DEPTH:(hk + 1) * QK8_DEPTH]

    v_t = qkv_t[Q_W + KV_W:].astype(BF16).reshape(N_KV_HEADS, HEAD_DIM, tm)
    kn_out[0, :, 1:2] = jnp.max(jnp.abs(v_t.astype(F32)), axis=1, keepdims=True)
    for c in range(tm // tk):
        v_out[0, :, c] = v_t[:, :, c * tk:(c + 1) * tk]

    f = jnp.dot(h, wf_ref[...], preferred_element_type=F32)
    z = jnp.dot(f.astype(BF16), cs_ref[...], preferred_element_type=F32)
    zr_out[0] = z[:, :F_W]
    zi_out[0] = z[:, F_W:]


def _proj_call(x, g, wqkv_t, wf, qg, kg, cos_t, sin_t, cs, *, tm, tk):
    B, S, D = x.shape
    n_rows = wqkv_t.shape[0]
    full = lambda shape: pl.BlockSpec(shape, lambda b, i: (0,) * len(shape))
    return pl.pallas_call(
        functools.partial(_proj_kernel, tk=tk),
        grid=(B, S // tm),
        in_specs=[
            pl.BlockSpec((1, tm, D), lambda b, i: (b, i, 0)),
            full((1, D)),
            full((n_rows, D)),
            full((D, F_W)),
            full((HEAD_DIM, 1)),
            full((HEAD_DIM, 1)),
            pl.BlockSpec((HALF, tm), lambda b, i: (0, i)),
            pl.BlockSpec((HALF, tm), lambda b, i: (0, i)),
            full((F_W, 2 * F_W)),
        ],
        out_specs=[
            pl.BlockSpec((1, N_KV_HEADS, Q_GROUP, HEAD_DIM, tm), lambda b, i: (b, 0, 0, 0, i)),
            pl.BlockSpec((1, N_KV_HEADS, tm, HEAD_DIM), lambda b, i: (b, 0, i, 0)),
            pl.BlockSpec((1, N_KV_HEADS, Q_GROUP, QK8_DEPTH, tm), lambda b, i: (b, 0, 0, 0, i)),
            pl.BlockSpec((1, N_KV_HEADS, tm, QK8_DEPTH), lambda b, i: (b, 0, i, 0)),
            pl.BlockSpec((1, N_KV_HEADS, 2, tm), lambda b, i: (b, 0, 0, i)),
            pl.BlockSpec((1, N_KV_HEADS, tm // tk, HEAD_DIM, tk), lambda b, i: (b, 0, i, 0, 0)),
            pl.BlockSpec((1, tm, F_W), lambda b, i: (b, i, 0)),
            pl.BlockSpec((1, tm, F_W), lambda b, i: (b, i, 0)),
        ],
        out_shape=[
            jax.ShapeDtypeStruct((B, N_KV_HEADS, Q_GROUP, HEAD_DIM, S), BF16),
            jax.ShapeDtypeStruct((B, N_KV_HEADS, S, HEAD_DIM), BF16),
            jax.ShapeDtypeStruct((B, N_KV_HEADS, Q_GROUP, QK8_DEPTH, S), F8),
            jax.ShapeDtypeStruct((B, N_KV_HEADS, S, QK8_DEPTH), F8),
            jax.ShapeDtypeStruct((B, N_KV_HEADS, 2, S), F32),
            jax.ShapeDtypeStruct((B, N_KV_HEADS, S // tk, HEAD_DIM, tk), BF16),
            jax.ShapeDtypeStruct((B, S, F_W), F32),
            jax.ShapeDtypeStruct((B, S, F_W), F32),
        ],
        compiler_params=_params("parallel", "parallel"),
        name="proj",
    )(x, g, wqkv_t, wf, qg, kg, cos_t, sin_t, cs)


def _attn_kernel(q_ref, k_ref, q8_ref, k8_ref, v_ref, kn_ref, o_ref, s_buf, cmax_ref, m_ref, acc_ref,
                 *, tk, n_chunks):
    tq = q_ref.shape[-1]
    q_cat = jnp.concatenate([q_ref[0, 0, g] for g in range(Q_GROUP)], axis=1)
    ones = jnp.ones((ONES_ROWS, tk), BF16)

    def scores(j, slot):
        start = pl.multiple_of(j * tk, tk)
        kc = k_ref[0, 0, pl.ds(start, tk), :]
        s_t = jnp.dot(kc, q_cat, preferred_element_type=F32)
        s_buf[slot] = s_t
        cmax_ref[slot] = jnp.max(s_t, axis=0, keepdims=True)

    def accumulate(j, slot):
        m_old = m_ref[...]
        m_new = jnp.maximum(m_old, cmax_ref[slot])
        alpha = jnp.exp2(m_old - m_new)
        p_t = jnp.exp2(s_buf[slot] - m_new).astype(BF16)
        v_ext = jnp.concatenate([v_ref[0, 0, j], ones], axis=0)
        acc_ref[...] = acc_ref[...] * alpha + jnp.dot(v_ext, p_t, preferred_element_type=F32)
        m_ref[...] = m_new

    def general_path():
        m_ref[...] = jnp.full(m_ref.shape, -jnp.inf, F32)
        scores(0, 0)

        def two_chunks(jj, carry):
            j0 = 2 * jj
            scores(j0 + 1, 1)
            accumulate(j0, 0)
            scores(jnp.minimum(j0 + 2, n_chunks - 1), 0)
            accumulate(j0 + 1, 1)
            return carry

        lax.fori_loop(0, n_chunks // 2, two_chunks, 0)

    def bounded_path():
        per_trip = math.gcd(BOUNDED_CHUNKS_PER_TRIP, n_chunks)
        q8_cat = jnp.concatenate([q8_ref[0, 0, g] for g in range(Q_GROUP)], axis=1)

        n_lane_tiles = q8_cat.shape[1] // ATTN_LANE_TILE
        n_tiles = per_trip * n_lane_tiles

        def trip(jj, carry):
            probs = {}

            def emit_scores(t):
                c, n = divmod(t, n_lane_tiles)
                start = pl.multiple_of((jj * per_trip + c) * tk, tk)
                kc = k8_ref[0, 0, pl.ds(start, tk), :]
                q_tile = q8_cat[:, n * ATTN_LANE_TILE:(n + 1) * ATTN_LANE_TILE]
                s_t = jnp.dot(kc, q_tile, preferred_element_type=F32) * (1.0 / Q8_PRESCALE)
                probs[t] = jnp.exp2(s_t).astype(BF16)

            def emit_pv(t):
                c, n = divmod(t, n_lane_tiles)
                v_ext = jnp.concatenate([v_ref[0, 0, jj * per_trip + c], ones], axis=0)
                lanes = slice(n * ATTN_LANE_TILE, (n + 1) * ATTN_LANE_TILE)
                acc_ref[:, lanes] += jnp.dot(v_ext, probs.pop(t), preferred_element_type=F32)

            for t in range(min(BOUNDED_SCORE_LOOKAHEAD, n_tiles)):
                emit_scores(t)
            for t in range(n_tiles):
                emit_pv(t)
                if t + BOUNDED_SCORE_LOOKAHEAD < n_tiles:
                    emit_scores(t + BOUNDED_SCORE_LOOKAHEAD)
            return carry

        lax.fori_loop(0, n_chunks // per_trip, trip, 0)

    acc_ref[...] = jnp.zeros(acc_ref.shape, F32)
    q32 = q_cat.astype(F32)
    q_norm2 = jnp.max(jnp.sum(q32 * q32, axis=0, keepdims=True))
    k_norm2 = jnp.max(kn_ref[0, 0, 0:1])
    v_max = jnp.max(kn_ref[0, 0, 1:2])
    bounded = jnp.logical_and(q_norm2 * k_norm2 <= SAFE_SCORE_BOUND ** 2, v_max <= SAFE_VALUE_BOUND)
    fits_f8 = jnp.logical_and(q_norm2 <= (F8_SAFE_ABS / Q8_PRESCALE) ** 2, k_norm2 <= F8_SAFE_ABS ** 2)
    lax.cond(jnp.logical_and(bounded, fits_f8), bounded_path, general_path)
    acc = acc_ref[...]
    out = acc[:HEAD_DIM] / acc[HEAD_DIM:HEAD_DIM + 1]
    for g in range(Q_GROUP):
        o_ref[0, 0, g] = out[:, g * tq:(g + 1) * tq].astype(o_ref.dtype)


def _attn_call(q_t, k, q8_t, k8, v_t, k_norm2, *, tq):
    B, _, _, _, S = q_t.shape
    n_chunks, tk = v_t.shape[2], v_t.shape[4]
    return pl.pallas_call(
        functools.partial(_attn_kernel, tk=tk, n_chunks=n_chunks),
        grid=(B, N_KV_HEADS, S // tq),
        in_specs=[
            pl.BlockSpec((1, 1, Q_GROUP, HEAD_DIM, tq), lambda b, h, i: (b, h, 0, 0, i)),
            pl.BlockSpec((1, 1, S, HEAD_DIM), lambda b, h, i: (b, h, 0, 0)),
            pl.BlockSpec((1, 1, Q_GROUP, QK8_DEPTH, tq), lambda b, h, i: (b, h, 0, 0, i)),
            pl.BlockSpec((1, 1, S, QK8_DEPTH), lambda b, h, i: (b, h, 0, 0)),
            pl.BlockSpec((1, 1, n_chunks, HEAD_DIM, tk), lambda b, h, i: (b, h, 0, 0, 0)),
            pl.BlockSpec((1, 1, 2, S), lambda b, h, i: (b, h, 0, 0)),
        ],
        out_specs=pl.BlockSpec((1, 1, Q_GROUP, HEAD_DIM, tq), lambda b, h, i: (b, h, 0, 0, i)),
        out_shape=jax.ShapeDtypeStruct((B, N_KV_HEADS, Q_GROUP, HEAD_DIM, S), BF16),
        scratch_shapes=[
            pltpu.VMEM((2, tk, Q_GROUP * tq), F32),
            pltpu.VMEM((2, 1, Q_GROUP * tq), F32),
            pltpu.VMEM((1, Q_GROUP * tq), F32),
            pltpu.VMEM((HEAD_DIM + ONES_ROWS, Q_GROUP * tq), F32),
        ],
        compiler_params=_params("parallel", "parallel", "parallel"),
        name="attn",
    )(q_t, k, q8_t, k8, v_t, k_norm2)


def _fft1_kernel(zr_ref, zi_ref, c1_ref, s1_ref, tc_ref, ts_ref, yr_out, yi_out):
    n1, span, width = zr_ref.shape[1:]
    sub = FFT1_SUBLANE_BLOCK
    rows = n1 * sub
    c1, s1 = c1_ref[...], s1_ref[...]
    dot = functools.partial(jnp.dot, preferred_element_type=F32)
    for t in range(span // sub):
        part = slice(t * sub, (t + 1) * sub)
        zr = zr_ref[0, :, part, :].reshape(rows, width).astype(BF16)
        zi = zi_ref[0, :, part, :].reshape(rows, width).astype(BF16)
        yr = dot(c1, zr) + dot(s1, zi)
        yi = dot(c1, zi) - dot(s1, zr)
        tc = tc_ref[:, part, :].reshape(rows, width)
        ts = ts_ref[:, part, :].reshape(rows, width)
        yr_out[0, :, part, :] = (yr * tc + yi * ts).reshape(n1, sub, width)
        yi_out[0, :, part, :] = (yi * tc - yr * ts).reshape(n1, sub, width)


def _fft1_call(zr, zi, c1, s1, tc, ts):
    B, n1, n2, C = zr.shape
    sub = math.gcd(FFT1_POSITIONS_PER_STEP, n2)
    blk = pl.BlockSpec((1, n1, sub, C), lambda i, b: (b, 0, i, 0))
    tw = pl.BlockSpec((n1, sub, C), lambda i, b: (0, i, 0))
    full = pl.BlockSpec(c1.shape, lambda i, b: (0, 0))
    return pl.pallas_call(
        _fft1_kernel,
        grid=(n2 // sub, B),
        in_specs=[blk, blk, full, full, tw, tw],
        out_specs=[blk, blk],
        out_shape=[jax.ShapeDtypeStruct((B, n1, n2, C), F32)] * 2,
        compiler_params=_params("parallel", "parallel"),
        name="fft_stage1",
    )(zr, zi, c1, s1, tc, ts)


def _fft2_kernel(yr_ref, yi_ref, c2_ref, s2_ref, o_ref, *, group):
    c2, s2 = c2_ref[...], s2_ref[...]
    width = yr_ref.shape[-1]
    for g in range(group):
        yr = yr_ref[0, g].astype(BF16)
        yi = yi_ref[0, g].astype(BF16)
        re = (jnp.dot(c2, yr, preferred_element_type=F32)
              + jnp.dot(s2, yi, preferred_element_type=F32))
        o_ref[0, :, g, :] = re


def _fft2_call(yr, yi, c2, s2, *, group):
    B, n1, n2, C = yr.shape
    blk = pl.BlockSpec((1, group, n2, C), lambda b, i: (b, i, 0, 0))
    full = pl.BlockSpec((n2, n2), lambda b, i: (0, 0))
    return pl.pallas_call(
        functools.partial(_fft2_kernel, group=group),
        grid=(B, n1 // group),
        in_specs=[blk, blk, full, full],
        out_specs=pl.BlockSpec((1, n2, group, C), lambda b, i: (b, 0, i, 0)),
        out_shape=jax.ShapeDtypeStruct((B, n2, n1, C), F32),
        compiler_params=_params("parallel", "parallel"),
        name="fft_stage2",
    )(yr, yi, c2, s2)


def _merge_kernel(x_ref, g_ref, at_ref, four_ref, wgate_ref, bg_ref, wab_ref, wfb_ref, wout_ref, o_ref):
    D = x_ref.shape[-1]
    x = x_ref[0]
    ms = jnp.mean(x * x, axis=-1, keepdims=True)
    h = (x * lax.rsqrt(ms + EPS) * g_ref[...]).astype(BF16)
    gates = jax.nn.sigmoid(jnp.dot(h, wgate_ref[...], preferred_element_type=F32) + bg_ref[...])
    a = lax.dot_general(at_ref[0], wab_ref[...], (((0,), (0,)), ((), ())),
                        preferred_element_type=F32)
    f = jnp.dot(four_ref[0].astype(BF16), wfb_ref[...], preferred_element_type=F32)
    merged = (gates[:, :D] * a + gates[:, D:] * f).astype(BF16)
    o_ref[0] = x + jnp.dot(merged, wout_ref[...], preferred_element_type=F32)


def _merge_call(x, g, attn_t, four, wgate, bg, wab, wfb, wout, *, tm):
    B, S, D = x.shape
    full = lambda shape: pl.BlockSpec(shape, lambda b, i: (0,) * len(shape))
    row = lambda w: pl.BlockSpec((1, tm, w), lambda b, i: (b, i, 0))
    return pl.pallas_call(
        _merge_kernel,
        grid=(B, S // tm),
        in_specs=[
            row(D),
            full((1, D)),
            pl.BlockSpec((1, Q_W, tm), lambda b, i: (b, 0, i)),
            row(F_W),
            full((D, 2 * D)), full((1, 2 * D)),
            full((Q_W, D)), full((F_W, D)), full((D, D)),
        ],
        out_specs=row(D),
        out_shape=jax.ShapeDtypeStruct((B, S, D), F32),
        compiler_params=_params("parallel", "parallel"),
        name="merge",
    )(x, g, attn_t, four, wgate, bg, wab, wfb, wout)


def _block_tail_kernel(x_ref, g_ref, at_ref, four_ref, wgate_ref, bg_ref, wab_ref, wfb_ref, wout_ref,
                       g2_ref, wup_ref, wdown_ref, gf_ref, o_ref, *, ff_chunk, final_norm):
    _merge_kernel(x_ref, g_ref, at_ref, four_ref, wgate_ref, bg_ref, wab_ref, wfb_ref, wout_ref, o_ref)
    _mlp_kernel(o_ref, g2_ref, wup_ref, wdown_ref, gf_ref, o_ref, ff_chunk=ff_chunk, final_norm=final_norm)


def _block_tail_call(x, g, attn_t, four, wgate, bg, wab, wfb, wout, g2, wup, wdown, gf, *, tm, final_norm):
    B, S, D = x.shape
    d_ff = wup.shape[1]
    full = lambda shape: pl.BlockSpec(shape, lambda b, i: (0,) * len(shape), pipeline_mode=pl.Buffered(1))
    row = lambda w: pl.BlockSpec((1, tm, w), lambda b, i: (b, i, 0))
    return pl.pallas_call(
        functools.partial(_block_tail_kernel, ff_chunk=min(1024, d_ff), final_norm=final_norm),
        grid=(B, S // tm),
        in_specs=[
            row(D),
            full((1, D)),
            pl.BlockSpec((1, Q_W, tm), lambda b, i: (b, 0, i)),
            row(F_W),
            full((D, 2 * D)), full((1, 2 * D)),
            full((Q_W, D)), full((F_W, D)), full((D, D)),
            full((1, D)), full((D, d_ff)), full((d_ff, D)), full((1, D)),
        ],
        out_specs=row(D),
        out_shape=jax.ShapeDtypeStruct((B, S, D), F32),
        compiler_params=_params("parallel", "parallel"),
        name="block_tail",
    )(x, g, attn_t, four, wgate, bg, wab, wfb, wout, g2, wup, wdown, gf)


def _mlp_kernel(x_ref, g_ref, wup_ref, wdown_ref, gf_ref, o_ref, *, ff_chunk, final_norm):
    x = x_ref[0]
    ms = jnp.mean(x * x, axis=-1, keepdims=True)
    h = (x * lax.rsqrt(ms + EPS) * g_ref[...]).astype(BF16)
    d_ff = wup_ref.shape[1]
    y = x
    for c in range(d_ff // ff_chunk):
        u = jnp.maximum(jnp.dot(h, wup_ref[:, c * ff_chunk:(c + 1) * ff_chunk],
                                preferred_element_type=F32), 0.0)
        y = y + jnp.dot((u * u).astype(BF16), wdown_ref[c * ff_chunk:(c + 1) * ff_chunk, :],
                        preferred_element_type=F32)
    if final_norm:
        ms = jnp.mean(y * y, axis=-1, keepdims=True)
        y = y * lax.rsqrt(ms + EPS) * gf_ref[...]
    o_ref[0] = y


def _mlp_call(x, g, wup, wdown, gf, *, tm, final_norm):
    B, S, D = x.shape
    d_ff = wup.shape[1]
    full = lambda shape: pl.BlockSpec(shape, lambda b, i: (0,) * len(shape))
    row = pl.BlockSpec((1, tm, D), lambda b, i: (b, i, 0))
    return pl.pallas_call(
        functools.partial(_mlp_kernel, ff_chunk=min(1024, d_ff), final_norm=final_norm),
        grid=(B, S // tm),
        in_specs=[row, full((1, D)), full((D, d_ff)), full((d_ff, D)), full((1, D))],
        out_specs=row,
        out_shape=jax.ShapeDtypeStruct((B, S, D), F32),
        compiler_params=_params("parallel", "parallel"),
        name="mlp",
    )(x, g, wup, wdown, gf)


def _rope_tables_t(seq):
    n_rows = seq // GRID_W
    rows = jnp.repeat(jnp.arange(n_rows, dtype=F32), GRID_W)
    cols = jnp.tile(jnp.arange(GRID_W, dtype=F32), n_rows)
    inv_freq = ROPE_THETA ** (-jnp.arange(ROPE_PAIRS_PER_AXIS, dtype=F32) / ROPE_PAIRS_PER_AXIS)
    ang = jnp.concatenate([rows[:, None] * inv_freq[None, :], cols[:, None] * inv_freq[None, :]], axis=-1)
    return jnp.cos(ang).T, jnp.sin(ang).T


def _dft_cos_sin(n, scale):
    idx = np.arange(n)
    ang = 2.0 * np.pi * ((idx[:, None] * idx[None, :]) % n) / n
    return np.cos(ang) * scale, np.sin(ang) * scale


def _fourier_constants(seq, n1, n2):
    gc, gs = _dft_cos_sin(FOURIER_GROUP_W, FOURIER_GROUP_W ** -0.5)
    eye = np.eye(N_FOURIER_GROUPS)
    cs = np.concatenate([np.kron(eye, gc), -np.kron(eye, gs)], axis=1)
    c1, s1 = (np.kron(m, np.eye(FFT1_SUBLANE_BLOCK)) for m in _dft_cos_sin(n1, n1 ** -0.5))
    c2, s2 = _dft_cos_sin(n2, n2 ** -0.5)
    k1 = np.arange(n1)[:, None]
    s2_idx = np.arange(n2)[None, :]
    ang = 2.0 * np.pi * ((k1 * s2_idx) % seq) / seq
    tc = np.repeat(np.cos(ang)[:, :, None], F_W, axis=2)
    ts = np.repeat(np.sin(ang)[:, :, None], F_W, axis=2)
    f32 = lambda a: jnp.asarray(a, dtype=F32)
    bf = lambda a: f32(a).astype(BF16)
    return bf(cs), bf(c1), bf(s1), bf(c2), bf(s2), f32(tc), f32(ts)


def _pair_split_perm():
    return np.concatenate([np.arange(0, HEAD_DIM, 2), np.arange(1, HEAD_DIM, 2)])


def kernel(x, norm_mix, w_in, b_gate, q_gain, k_gain, w_attn_branch, w_fourier_branch, w_out,
           norm_mlp, w_up, w_down, norm_final):
    B, S, D = x.shape
    depth = w_in.shape[0]
    tm, tp, tk, tq, n1, n2 = _tiles(S)

    perm = _pair_split_perm()
    rot_w = Q_W + KV_W
    fg_start = Q_W + 2 * KV_W

    w_in16 = w_in.astype(BF16)
    w_rot = w_in16[:, :, :rot_w].reshape(depth, D, rot_w // HEAD_DIM, HALF, 2)
    w_rot_t = jnp.transpose(w_rot, (0, 2, 4, 3, 1)).reshape(depth, rot_w, D)
    w_v_t = jnp.swapaxes(w_in16[:, :, rot_w:fg_start], 1, 2)
    wqkv_t_all = jnp.concatenate([w_rot_t, w_v_t], axis=1)

    cos_t, sin_t = _rope_tables_t(S)
    cs, c1, s1, c2, s2, tc, ts = _fourier_constants(S, n1, n2)

    for l in range(depth):
        wqkv_t = wqkv_t_all[l]
        wf = w_in16[l][:, fg_start:fg_start + F_W]
        wgate = w_in16[l][:, fg_start + F_W:]
        qg = q_gain[l][perm][:, None]
        kg = k_gain[l][perm][:, None]
        q_t, k, q8_t, k8, k_norm2, v_t, zr, zi = _proj_call(
            x, norm_mix[l][None], wqkv_t, wf, qg, kg, cos_t, sin_t, cs, tm=tp, tk=tk)

        attn_t = _attn_call(q_t, k, q8_t, k8, v_t, k_norm2, tq=tq)
        attn_t = attn_t.reshape(B, Q_W, S)

        yr, yi = _fft1_call(zr.reshape(B, n1, n2, F_W), zi.reshape(B, n1, n2, F_W), c1, s1, tc, ts)
        four = _fft2_call(yr, yi, c2, s2, group=min(8, n1))
        four = four.reshape(B, S, F_W)

        x = _block_tail_call(x, norm_mix[l][None], attn_t, four, wgate, b_gate[l][None],
                             w_attn_branch[l].astype(BF16), w_fourier_branch[l].astype(BF16),
                             w_out[l].astype(BF16), norm_mlp[l][None], w_up[l].astype(BF16),
                             w_down[l].astype(BF16), norm_final[None], tm=tm,
                             final_norm=(l == depth - 1))
    return x
```
